```python
import math
import jax, jax.numpy as jnp
from jax import lax
import numpy as np

D_MODEL = 2048
BATCH = 8
SEQ = 4096
DEPTH = 4

HEAD_DIM = 128
Q_BLOCK = 128
EPS = 1e-6
A_HEADS = D_MODEL // 512
A_WIDTH = A_HEADS * 2 * HEAD_DIM
B_HEADS = D_MODEL // 256
B_WIDTH = B_HEADS * HEAD_DIM
EVEN_SPLITS = (A_WIDTH, 2 * A_WIDTH, 3 * A_WIDTH, 3 * A_WIDTH + B_WIDTH, 3 * A_WIDTH + 2 * B_WIDTH)
EVEN_IN = 3 * A_WIDTH + 3 * B_WIDTH
EVEN_MIX = A_WIDTH + B_WIDTH
C_HEADS = D_MODEL // 256
C_WIDTH = C_HEADS * HEAD_DIM
DILATED = ((128, 1), (512, 4), (2048, 16))
D_CH = D_MODEL // 2
CONV_W = 31
ODD_SPLITS = (C_WIDTH, 2 * C_WIDTH, 3 * C_WIDTH)
ODD_IN = 3 * C_WIDTH + 2 * D_CH
ODD_MIX = C_WIDTH + D_CH
D_FF = ((8 * D_MODEL // 3 + 255) // 256) * 256
N_EXPERTS = 8
TOP_K = 2

kernel_name = "hybrid_diffattn_stickbreak_dilated_conformer_moe"


def rms_norm(x, g):
    xf = x.astype(jnp.float32)
    y = xf * lax.rsqrt(jnp.mean(xf * xf, axis=-1, keepdims=True) + EPS)
    return (y * g.astype(jnp.float32)).astype(x.dtype)


def layer_norm(x, g, b):
    xf = x.astype(jnp.float32)
    xc = xf - jnp.mean(xf, axis=-1, keepdims=True)
    y = xc * lax.rsqrt(jnp.mean(xc * xc, axis=-1, keepdims=True) + EPS)
    return (y * g.astype(jnp.float32) + b.astype(jnp.float32)).astype(x.dtype)


def alibi_slopes(n):
    return 2.0 ** (-8.0 * jnp.arange(1, n + 1, dtype=jnp.float32) / n)


def to_blocks(a):
    B, T = a.shape[:2]
    return a.reshape(B, T // Q_BLOCK, Q_BLOCK, *a.shape[2:]).swapaxes(0, 1)


def from_blocks(a):
    a = a.swapaxes(0, 1)
    return a.reshape(a.shape[0], a.shape[1] * a.shape[2], *a.shape[3:])


def differential_attention(q1, q2, k1, k2, v, lam, slopes):
    T, dh = q1.shape[1], q1.shape[-1]
    scale = dh ** -0.5
    kpos = jnp.arange(T)

    def one_block(args):
        i, qa, qb = args
        qpos = i * Q_BLOCK + jnp.arange(Q_BLOCK)
        dist = (qpos[:, None] - kpos[None, :]).astype(jnp.float32)
        bias = jnp.where(dist[None] >= 0, -slopes[:, None, None] * dist[None], -jnp.inf)

        def probs(qq, kk):
            s = jnp.einsum('bqhd,bkhd->bhqk', qq, kk).astype(jnp.float32) * scale + bias
            return jax.nn.softmax(s, axis=-1)

        p = probs(qa, k1) - lam * probs(qb, k2)
        return jnp.einsum('bhqk,bkhe->bqhe', p.astype(v.dtype), v)

    nb = T // Q_BLOCK
    out = lax.map(one_block, (jnp.arange(nb), to_blocks(q1), to_blocks(q2)))
    return from_blocks(out)


def stick_breaking_attention(q, k, v):
    T, dh = q.shape[1], q.shape[-1]
    scale = dh ** -0.5
    kpos = jnp.arange(T)

    def one_block(args):
        i, qb = args
        qpos = i * Q_BLOCK + jnp.arange(Q_BLOCK)
        past = kpos[None, :] < qpos[:, None]
        z = jnp.einsum('bqhd,bkhd->bhqk', qb, k).astype(jnp.float32) * scale
        log_beta = jax.nn.log_sigmoid(z)
        log_1m_beta = jnp.where(past, jax.nn.log_sigmoid(-z), 0.0)
        later = lax.cumsum(log_1m_beta, axis=3, reverse=True) - log_1m_beta
        w = jnp.where(past, jnp.exp(log_beta + later), 0.0)
        return jnp.einsum('bhqk,bkhd->bqhd', w.astype(v.dtype), v)

    nb = T // Q_BLOCK
    out = lax.map(one_block, (jnp.arange(nb), to_blocks(q)))
    return from_blocks(out)


def dilated_branch(q, k, v, window, dilation, slopes):
    B, T, H, dh = q.shape
    steps = window // dilation
    L = T // dilation
    nb = -(-L // Q_BLOCK)
    Lp = nb * Q_BLOCK

    def to_sub(a):
        e = a.shape[-1]
        a = a.reshape(B, L, dilation, H, e).transpose(0, 2, 1, 3, 4).reshape(B * dilation, L, H, e)
        return jnp.pad(a, ((0, 0), (0, Lp - L), (0, 0), (0, 0)))

    def band(a):
        X, e = a.shape[0], a.shape[-1]
        a = jnp.pad(a, ((0, 0), (Q_BLOCK, 0), (0, 0), (0, 0))).reshape(X, nb + 1, Q_BLOCK, H, e)
        return jnp.concatenate([a[:, :-1], a[:, 1:]], axis=2)

    def from_sub(a):
        e = a.shape[-1]
        a = a.reshape(B, dilation, Lp, H, e)[:, :, :L]
        return a.transpose(0, 2, 1, 3, 4).reshape(B, T, H, e)

    qb = to_sub(q).reshape(B * dilation, nb, Q_BLOCK, H, dh)
    kb, vb = band(to_sub(k)), band(to_sub(v))
    n_q = jnp.arange(nb)[:, None, None] * Q_BLOCK + jnp.arange(Q_BLOCK)[None, :, None]
    n_k = (jnp.arange(nb)[:, None, None] - 1) * Q_BLOCK + jnp.arange(2 * Q_BLOCK)[None, None, :]
    dist = n_q - n_k
    valid = (dist >= 0) & (dist <= steps) & (n_k >= 0)
    real_dist = (dist * dilation).astype(jnp.float32)
    bias = jnp.where(valid[:, None], -slopes[:, None, None] * real_dist[:, None], -jnp.inf)
    s = jnp.einsum('xnqhd,xnkhd->xnhqk', qb, kb).astype(jnp.float32) * dh ** -0.5 + bias
    m = jnp.max(s, axis=-1)
    p = jnp.exp(s - m[..., None])
    l = jnp.sum(p, axis=-1)
    acc = jnp.einsum('xnhqk,xnkhe->xnqhe', p.astype(v.dtype), vb).astype(jnp.float32)
    m = from_sub(m.transpose(0, 1, 3, 2)[..., None])
    l = from_sub(l.transpose(0, 1, 3, 2)[..., None])
    return m, l, from_sub(acc)


def dilated_mixture(q, k, v, slopes):
    parts = [dilated_branch(q, k, v, w, d, slopes) for (w, d) in DILATED]
    m_all = jnp.max(jnp.stack([p[0] for p in parts]), axis=0)
    num = jnp.zeros(q.shape, jnp.float32)
    den = jnp.zeros(q.shape[:-1] + (1,), jnp.float32)
    for m, l, acc in parts:
        r = jnp.exp(m - m_all)
        num = num + acc * r
        den = den + l * r
    return (num / den).astype(q.dtype)


def conformer_conv(u, conv_w, conv_b, ln_g, ln_b):
    a, g = jnp.split(u, 2, axis=-1)
    h = a * jax.nn.sigmoid(g)
    h = lax.conv_general_dilated(h, conv_w[:, None, :].astype(h.dtype), window_strides=(1,),
                                 padding=[(CONV_W - 1, 0)], dimension_numbers=('NWC', 'WIO', 'NWC'),
                                 feature_group_count=D_CH) + conv_b
    return jax.nn.silu(layer_norm(h, ln_g, ln_b))


def even_mixer(h, w_in, w_out, q_g, k_g, lam_vecs, head_g, lam_init):
    B, T, _ = h.shape
    aq, ak, av, bq, bk, bv = jnp.split(h @ w_in, EVEN_SPLITS, axis=-1)
    aq = aq.reshape(B, T, A_HEADS, 2, HEAD_DIM)
    ak = ak.reshape(B, T, A_HEADS, 2, HEAD_DIM)
    q1, q2 = rms_norm(aq[..., 0, :], q_g), rms_norm(aq[..., 1, :], q_g)
    k1, k2 = rms_norm(ak[..., 0, :], k_g), rms_norm(ak[..., 1, :], k_g)
    lv = lam_vecs.astype(jnp.float32)
    lam = jnp.exp(jnp.sum(lv[0] * lv[1])) - jnp.exp(jnp.sum(lv[2] * lv[3])) + lam_init
    oa = differential_attention(q1, q2, k1, k2, av.reshape(B, T, A_HEADS, 2 * HEAD_DIM), lam,
                                alibi_slopes(A_HEADS))
    oa = rms_norm(oa, head_g) * (1.0 - lam_init)
    shp = (B, T, B_HEADS, HEAD_DIM)
    ob = stick_breaking_attention(bq.reshape(shp), bk.reshape(shp), bv.reshape(shp))
    mixed = jnp.concatenate([oa.reshape(B, T, A_WIDTH), ob.reshape(B, T, B_WIDTH)], axis=-1)
    return mixed @ w_out


def odd_mixer(h, w_in, w_out, q_g, k_g, conv_w, conv_b, ln_g, ln_b):
    B, T, _ = h.shape
    cq, ck, cv, du = jnp.split(h @ w_in, ODD_SPLITS, axis=-1)
    shp = (B, T, C_HEADS, HEAD_DIM)
    oc = dilated_mixture(rms_norm(cq.reshape(shp), q_g), rms_norm(ck.reshape(shp), k_g),
                         cv.reshape(shp), alibi_slopes(C_HEADS))
    od = conformer_conv(du, conv_w, conv_b, ln_g, ln_b)
    mixed = jnp.concatenate([oc.reshape(B, T, C_WIDTH), od], axis=-1)
    return mixed @ w_out


def swiglu(h, wg, wu, wd):
    return (jax.nn.silu(h @ wg) * (h @ wu)) @ wd


def moe_swiglu(h, router_w, wg, wu, wd):
    B, T, D = h.shape
    xt = h.reshape(B * T, D)
    logits = (xt @ router_w).astype(jnp.float32)
    top_vals, top_idx = lax.top_k(logits, TOP_K)
    top_w = jax.nn.softmax(top_vals, axis=-1)
    gates = jnp.sum(jax.nn.one_hot(top_idx, N_EXPERTS, dtype=jnp.float32) * top_w[..., None], axis=1)
    y = jnp.zeros_like(xt)
    for e in range(N_EXPERTS):
        y = y + gates[:, e:e + 1].astype(xt.dtype) * swiglu(xt, wg[e], wu[e], wd[e])
    return y.reshape(B, T, D)


def setup_inputs(seed: int = 0) -> dict:
    key = jax.random.key(seed)
    keys = iter(jax.random.split(key, 32))
    D = D_MODEL
    NE = (DEPTH + 1) // 2
    NO = DEPTH // 2

    def nrm(shape, s):
        return jax.random.normal(next(keys), shape, jnp.float32) * s

    return {
        "x": nrm((BATCH, SEQ, D), 1.0),
        "c": nrm((BATCH, D), 1.0),
        "ada_w": nrm((DEPTH, D, 6 * D), 0.5 * D ** -0.5),
        "ada_b": nrm((DEPTH, 6 * D), 0.01),
        "norm_mix_g": 1.0 + nrm((DEPTH, D), 0.02),
        "norm_ffn_g": 1.0 + nrm((DEPTH, D), 0.02),
        "ev_w_in": nrm((NE, D, EVEN_IN), D ** -0.5),
        "ev_w_out": nrm((NE, EVEN_MIX, D), EVEN_MIX ** -0.5),
        "a_q_norm_g": 1.0 + nrm((NE, HEAD_DIM), 0.02),
        "a_k_norm_g": 1.0 + nrm((NE, HEAD_DIM), 0.02),
        "a_lambda": nrm((NE, 4, HEAD_DIM), 0.1),
        "a_head_norm_g": 1.0 + nrm((NE, 2 * HEAD_DIM), 0.02),
        "ffn_w_gate": nrm((NE, D, D_FF), D ** -0.5),
        "ffn_w_up": nrm((NE, D, D_FF), D ** -0.5),
        "ffn_w_down": nrm((NE, D_FF, D), D_FF ** -0.5),
        "od_w_in": nrm((NO, D, ODD_IN), D ** -0.5),
        "od_w_out": nrm((NO, ODD_MIX, D), ODD_MIX ** -0.5),
        "c_q_norm_g": 1.0 + nrm((NO, HEAD_DIM), 0.02),
        "c_k_norm_g": 1.0 + nrm((NO, HEAD_DIM), 0.02),
        "d_conv_w": nrm((NO, CONV_W, D_CH), CONV_W ** -0.5),
        "d_conv_b": nrm((NO, D_CH), 0.01),
        "d_ln_g": 1.0 + nrm((NO, D_CH), 0.02),
        "d_ln_b": nrm((NO, D_CH), 0.01),
        "moe_router": nrm((NO, D, N_EXPERTS), D ** -0.5),
        "moe_w_gate": nrm((NO, N_EXPERTS, D, D_FF), D ** -0.5),
        "moe_w_up": nrm((NO, N_EXPERTS, D, D_FF), D ** -0.5),
        "moe_w_down": nrm((NO, N_EXPERTS, D_FF, D), D_FF ** -0.5),
    }


def reference(x, c, ada_w, ada_b, norm_mix_g, norm_ffn_g, ev_w_in, ev_w_out, a_q_norm_g,
              a_k_norm_g, a_lambda, a_head_norm_g, ffn_w_gate, ffn_w_up, ffn_w_down, od_w_in,
              od_w_out, c_q_norm_g, c_k_norm_g, d_conv_w, d_conv_b, d_ln_g, d_ln_b, moe_router,
              moe_w_gate, moe_w_up, moe_w_down):
    cond = jax.nn.silu(c)
    for i in range(DEPTH):
        j = i // 2
        ada = (cond @ ada_w[i] + ada_b[i])[:, None, :]
        sh1, sc1, g1, sh2, sc2, g2 = jnp.split(ada, 6, axis=-1)
        h = rms_norm(x, norm_mix_g[i]) * (1.0 + sc1) + sh1
        if i % 2 == 0:
            lam_init = 0.8 - 0.6 * math.exp(-0.3 * i)
            mix = even_mixer(h, ev_w_in[j], ev_w_out[j], a_q_norm_g[j], a_k_norm_g[j],
                             a_lambda[j], a_head_norm_g[j], lam_init)
        else:
            mix = odd_mixer(h, od_w_in[j], od_w_out[j], c_q_norm_g[j], c_k_norm_g[j],
                            d_conv_w[j], d_conv_b[j], d_ln_g[j], d_ln_b[j])
        x = x + g1 * mix
        h = rms_norm(x, norm_ffn_g[i]) * (1.0 + sc2) + sh2
        if i % 2 == 0:
            ff = swiglu(h, ffn_w_gate[j], ffn_w_up[j], ffn_w_down[j])
        else:
            ff = moe_swiglu(h, moe_router[j], moe_w_gate[j], moe_w_up[j], moe_w_down[j])
        x = x + g2 * ff
    return x
```

```python
import functools
import math

import jax
import jax.numpy as jnp
from jax import lax
from jax.experimental import pallas as pl
from jax.experimental.pallas import tpu as pltpu

F32 = jnp.float32
BF16 = jnp.bfloat16
EPS = 1e-6
HEAD_DIM = 128
LANES = 128
VMEM_LIMIT_BYTES = 56 * 1024 * 1024
TOP_K = 2
CONV_HALO = 32
STICK_UNDERFLOW = -104.0
DILATED = ((128, 1), (512, 4), (2048, 16))
NEG_INF = float("-inf")


def _cparams(*sem):
    return pltpu.CompilerParams(dimension_semantics=sem, vmem_limit_bytes=VMEM_LIMIT_BYTES)


def _dot(a, b):
    return jnp.dot(a, b, preferred_element_type=F32)


def _dot_nt(a, b):
    return lax.dot_general(a, b, (((1,), (1,)), ((), ())), preferred_element_type=F32)


def _silu(x):
    return x * (1.0 / (1.0 + jnp.exp(-x)))


def _norm_mod(x, g, sc, sh):
    ms = jnp.mean(x * x, axis=-1, keepdims=True)
    return (x * lax.rsqrt(ms + EPS) * g) * (1.0 + sc) + sh


def _ada_kernel(c_ref, w_ref, b_ref, o_ref):
    cond = _silu(c_ref[...])
    o_ref[...] = jnp.dot(cond, w_ref[...], preferred_element_type=F32,
                         precision=lax.Precision.HIGHEST) + b_ref[...]


def ada_all(c, ada_w, ada_b, tn=1024):
    depth, d, n = ada_w.shape
    b = c.shape[0]
    return pl.pallas_call(
        _ada_kernel,
        grid=(depth, n // tn),
        in_specs=[
            pl.BlockSpec((b, d), lambda i, j: (0, 0)),
            pl.BlockSpec((None, d, tn), lambda i, j: (i, 0, j)),
            pl.BlockSpec((None, 1, tn), lambda i, j: (i, 0, j)),
        ],
        out_specs=pl.BlockSpec((None, b, tn), lambda i, j: (i, 0, j)),
        out_shape=jax.ShapeDtypeStruct((depth, b, n), F32),
        compiler_params=_cparams("parallel", "parallel"),
        name="ada",
    )(c, ada_w, ada_b.reshape(depth, 1, n))


def _inproj_kernel(x_ref, g_ref, sc_ref, sh_ref, w_ref, hg_ref, o_ref, h_scr, *, n_norm_tiles):
    j = pl.program_id(1)

    @pl.when(j == 0)
    def _():
        h_scr[...] = _norm_mod(x_ref[...], g_ref[...], sc_ref[...], sh_ref[...]).astype(BF16)

    y = _dot(h_scr[...], w_ref[...])
    tn = y.shape[1]

    @pl.when(j < n_norm_tiles)
    def _():
        for c in range(tn // HEAD_DIM):
            sl = slice(c * HEAD_DIM, (c + 1) * HEAD_DIM)
            blk = y[:, sl]
            r = lax.rsqrt(jnp.mean(blk * blk, axis=-1, keepdims=True) + EPS)
            o_ref[:, sl] = (blk * r * hg_ref[:, sl]).astype(BF16)

    @pl.when(j >= n_norm_tiles)
    def _():
        o_ref[...] = y.astype(BF16)


def in_proj(x2, norm_g, ada3, sc_idx, sh_idx, w_bf16, head_gain, seq, tm=512, tn=1024):
    m, d = x2.shape
    n = w_bf16.shape[1]
    n_norm_tiles = head_gain.shape[1] // tn
    tiles_per_batch = seq // tm
    return pl.pallas_call(
        functools.partial(_inproj_kernel, n_norm_tiles=n_norm_tiles),
        grid=(m // tm, n // tn),
        in_specs=[
            pl.BlockSpec((tm, d), lambda i, j: (i, 0)),
            pl.BlockSpec((1, d), lambda i, j: (0, 0)),
            pl.BlockSpec((None, 1, d), lambda i, j: (i // tiles_per_batch, 0, sc_idx)),
            pl.BlockSpec((None, 1, d), lambda i, j: (i // tiles_per_batch, 0, sh_idx)),
            pl.BlockSpec((d, tn), lambda i, j: (0, j)),
            pl.BlockSpec((1, tn), lambda i, j: (0, jnp.minimum(j, n_norm_tiles - 1))),
        ],
        out_specs=pl.BlockSpec((tm, tn), lambda i, j: (i, j)),
        out_shape=jax.ShapeDtypeStruct((m, n), BF16),
        scratch_shapes=[pltpu.VMEM((tm, d), BF16)],
        compiler_params=_cparams("parallel", "arbitrary"),
        name="in_proj",
    )(x2, norm_g.reshape(1, d), ada3, ada3, w_bf16, head_gain)


def _diff_attn_kernel(slopes_ref, lam_ref, q_ref, k_ref, v_ref, hg_ref, o_ref,
                      m1_scr, l1_scr, a1_scr, m2_scr, l2_scr, a2_scr, *, tq, tk, out_scale):
    h = pl.program_id(1)
    qi = pl.program_id(2)
    slope = slopes_ref[h]
    lam = lam_ref[0]
    q = q_ref[...]
    q1 = q[:, :HEAD_DIM]
    q2 = q[:, HEAD_DIM:]
    rc = (lax.broadcasted_iota(jnp.int32, (tq, tk), 0)
          - lax.broadcasted_iota(jnp.int32, (tq, tk), 1)).astype(F32)

    for m_scr, l_scr, a_scr in ((m1_scr, l1_scr, a1_scr), (m2_scr, l2_scr, a2_scr)):
        m_scr[...] = jnp.full(m_scr.shape, NEG_INF, F32)
        l_scr[...] = jnp.zeros(l_scr.shape, F32)
        a_scr[...] = jnp.zeros(a_scr.shape, F32)

    n_kb = (qi * tq) // tk + tq // tk

    def body(kb, carry):
        ks = pl.multiple_of(kb * tk, tk)
        kblk = k_ref[pl.ds(ks, tk), :]
        vblk = v_ref[pl.ds(ks, tk), :]
        dist = rc + (qi * tq - kb * tk).astype(F32)
        bias = jnp.where(dist >= 0.0, -slope * dist, NEG_INF)

        def update(qq, kk, m_scr, l_scr, a_scr):
            s = _dot_nt(qq, kk) + bias
            m_old = m_scr[...]
            m_new = jnp.maximum(m_old, jnp.max(s, axis=-1, keepdims=True))
            alpha = jnp.exp(m_old - m_new)
            p = jnp.exp(s - m_new)
            l_scr[...] = alpha * l_scr[...] + jnp.sum(p, axis=-1, keepdims=True)
            a_scr[...] = alpha * a_scr[...] + _dot(p.astype(BF16), vblk)
            m_scr[...] = m_new

        update(q1, kblk[:, :HEAD_DIM], m1_scr, l1_scr, a1_scr)
        update(q2, kblk[:, HEAD_DIM:], m2_scr, l2_scr, a2_scr)
        return carry

    lax.fori_loop(0, n_kb, body, 0)

    o = a1_scr[...] / l1_scr[...] - lam * (a2_scr[...] / l2_scr[...])
    r = lax.rsqrt(jnp.mean(o * o, axis=-1, keepdims=True) + EPS)
    o_ref[...] = (o * r * hg_ref[...] * out_scale).astype(BF16)


def diff_attention(y, slopes, lam, head_g, batch, seq, n_heads, k_col0, v_col0, out_scale, tq=256, tk=256):
    m = y.shape[0]
    hw = 2 * HEAD_DIM
    nq = seq // tq
    kernel = functools.partial(_diff_attn_kernel, tq=tq, tk=tk, out_scale=out_scale)
    return pl.pallas_call(
        kernel,
        grid=(batch, n_heads, nq),
        in_specs=[
            pl.BlockSpec(memory_space=pltpu.SMEM),
            pl.BlockSpec(memory_space=pltpu.SMEM),
            pl.BlockSpec((tq, hw), lambda b, h, qi: (b * nq + qi, h)),
            pl.BlockSpec((seq, hw), lambda b, h, qi: (b, k_col0 // hw + h)),
            pl.BlockSpec((seq, hw), lambda b, h, qi: (b, v_col0 // hw + h)),
            pl.BlockSpec((1, hw), lambda b, h, qi: (0, 0)),
        ],
        out_specs=pl.BlockSpec((tq, hw), lambda b, h, qi: (b * nq + qi, h)),
        out_shape=jax.ShapeDtypeStruct((m, n_heads * hw), BF16),
        scratch_shapes=[
            pltpu.VMEM((tq, 1), F32), pltpu.VMEM((tq, 1), F32), pltpu.VMEM((tq, hw), F32),
            pltpu.VMEM((tq, 1), F32), pltpu.VMEM((tq, 1), F32), pltpu.VMEM((tq, hw), F32),
        ],
        compiler_params=_cparams("parallel", "parallel", "arbitrary"),
        name="diff_attn",
    )(slopes, lam, y, y, y, head_g.reshape(1, hw))


def _stick_kernel(q_ref, k_ref, v_ref, o_ref, acc_scr, carry_scr, *, tq, tk, scale):
    qi = pl.program_id(2)
    q = (q_ref[...].astype(F32) * scale).astype(BF16)
    row = lax.broadcasted_iota(jnp.int32, (tq, tk), 0)
    col = lax.broadcasted_iota(jnp.int32, (tq, tk), 1)
    rc = row - col
    jj = lax.broadcasted_iota(jnp.int32, (tk, tk), 0)
    ss = lax.broadcasted_iota(jnp.int32, (tk, tk), 1)
    upper = jnp.where(jj > ss, 1.0, 0.0).astype(BF16)

    acc_scr[...] = jnp.zeros(acc_scr.shape, F32)
    carry_scr[...] = jnp.zeros(carry_scr.shape, F32)
    kb0 = (qi * tq + tq) // tk - 1

    def cond(state):
        kb, live = state
        return jnp.logical_and(kb >= 0, live > STICK_UNDERFLOW)

    def body(state):
        kb, _ = state
        ks = pl.multiple_of(kb * tk, tk)
        kblk = k_ref[pl.ds(ks, tk), :]
        vblk = v_ref[pl.ds(ks, tk), :]
        z = _dot_nt(q, kblk)
        lsig = jnp.minimum(z, 0.0) - jnp.log(1.0 + jnp.exp(-jnp.abs(z)))
        past = rc + (qi * tq - kb * tk) > 0
        l1m = jnp.where(past, lsig - z, 0.0)
        hi = l1m.astype(BF16)
        lo = (l1m - hi.astype(F32)).astype(BF16)
        carry = carry_scr[...]
        later = carry + _dot(hi, upper) + _dot(lo, upper)
        w = jnp.where(past, jnp.exp(lsig + later), 0.0)
        acc_scr[...] += _dot(w.astype(BF16), vblk)
        carry = carry + jnp.sum(l1m, axis=-1, keepdims=True)
        carry_scr[...] = carry
        return kb - 1, jnp.max(carry)

    lax.while_loop(cond, body, (kb0, jnp.float32(0.0)))
    o_ref[...] = acc_scr[...].astype(BF16)


def stick_attention(y, batch, seq, n_heads, q_col0, k_col0, v_col0, tq=256, tk=128):
    m = y.shape[0]
    nq = seq // tq
    hd = HEAD_DIM
    kernel = functools.partial(_stick_kernel, tq=tq, tk=tk, scale=hd ** -0.5)
    return pl.pallas_call(
        kernel,
        grid=(batch, n_heads, nq),
        in_specs=[
            pl.BlockSpec((tq, hd), lambda b, h, qi: (b * nq + qi, q_col0 // hd + h)),
            pl.BlockSpec((seq, hd), lambda b, h, qi: (b, k_col0 // hd + h)),
            pl.BlockSpec((seq, hd), lambda b, h, qi: (b, v_col0 // hd + h)),
        ],
        out_specs=pl.BlockSpec((tq, hd), lambda b, h, qi: (b * nq + qi, h)),
        out_shape=jax.ShapeDtypeStruct((m, n_heads * hd), BF16),
        scratch_shapes=[pltpu.VMEM((tq, hd), F32), pltpu.VMEM((tq, 1), F32)],
        compiler_params=_cparams("parallel", "parallel", "arbitrary"),
        name="stick_attn",
    )(y, y, y)


def _dilated_kernel(*refs, n_heads, dilation, has_state, final, qb):
    if has_state:
        q_ref, kc_ref, kp_ref, vc_ref, vp_ref, acc_in_ref, ml_in_ref = refs[:7]
        outs = refs[7:]
    else:
        q_ref, kc_ref, kp_ref, vc_ref, vp_ref = refs[:5]
        outs = refs[5:]
    nb = pl.program_id(2)
    row = lax.broadcasted_iota(jnp.int32, (qb, qb), 0)
    col = lax.broadcasted_iota(jnp.int32, (qb, qb), 1)
    rc = row - col
    valid_c = rc >= 0
    valid_p = rc <= jnp.where(nb > 0, 0, -2 * qb)
    dist_c = rc.astype(F32)
    dist_p = (rc + qb).astype(F32)
    lane = lax.broadcasted_iota(jnp.int32, (qb, LANES), 1)
    ml_out = jnp.zeros((qb, LANES), F32)

    for h in range(n_heads):
        sl = slice(h * HEAD_DIM, (h + 1) * HEAD_DIM)
        slope = 2.0 ** (-8.0 * (h + 1) / n_heads) * dilation
        qh = q_ref[:, sl]
        s_c = _dot_nt(qh, kc_ref[:, sl]) + jnp.where(valid_c, -slope * dist_c, NEG_INF)
        s_p = _dot_nt(qh, kp_ref[:, sl]) + jnp.where(valid_p, -slope * dist_p, NEG_INF)
        m = jnp.maximum(jnp.max(s_c, axis=-1, keepdims=True), jnp.max(s_p, axis=-1, keepdims=True))
        if has_state:
            m_in = ml_in_ref[:, h:h + 1]
            l_in = ml_in_ref[:, n_heads + h:n_heads + h + 1]
            m_new = jnp.maximum(m, m_in)
        else:
            m_new = m
        p_c = jnp.exp(s_c - m_new)
        p_p = jnp.exp(s_p - m_new)
        l = jnp.sum(p_c, axis=-1, keepdims=True) + jnp.sum(p_p, axis=-1, keepdims=True)
        acc = _dot(p_c.astype(BF16), vc_ref[:, sl]) + _dot(p_p.astype(BF16), vp_ref[:, sl])
        if has_state:
            r_in = jnp.exp(m_in - m_new)
            l = l + l_in * r_in
            acc = acc + acc_in_ref[:, sl] * r_in
        if final:
            outs[0][:, sl] = (acc / l).astype(BF16)
        else:
            outs[0][:, sl] = acc
            ml_out = jnp.where(lane == h, m_new, ml_out)
            ml_out = jnp.where(lane == n_heads + h, l, ml_out)
    if not final:
        outs[1][...] = ml_out


def dilated_branch(y, state, batch, seq, n_heads, dilation, final, qb=128):
    m, n = y.shape
    hw = n_heads * HEAD_DIM
    groups = n // hw
    ln = seq // dilation
    nblk = ln // qb
    yv = y.reshape(batch, ln, dilation * n)
    grid = (batch, dilation, nblk)

    def cur(g):
        return pl.BlockSpec((None, qb, hw), lambda b, r, i: (b, i, r * groups + g))

    def prev(g):
        return pl.BlockSpec((None, qb, hw), lambda b, r, i: (b, jnp.maximum(i - 1, 0), r * groups + g))

    in_specs = [cur(0), cur(1), prev(1), cur(2), prev(2)]
    args = [yv, yv, yv, yv, yv]
    acc_spec = pl.BlockSpec((None, qb, hw), lambda b, r, i: (b, i, r))
    ml_spec = pl.BlockSpec((None, qb, LANES), lambda b, r, i: (b, i, r))
    if state is not None:
        acc_in, ml_in = state
        in_specs += [acc_spec, ml_spec]
        args += [acc_in.reshape(batch, ln, dilation * hw), ml_in.reshape(batch, ln, dilation * LANES)]
    if final:
        out_specs = [acc_spec]
        out_shape = [jax.ShapeDtypeStruct((batch, ln, dilation * hw), BF16)]
    else:
        out_specs = [acc_spec, ml_spec]
        out_shape = [jax.ShapeDtypeStruct((batch, ln, dilation * hw), F32),
                     jax.ShapeDtypeStruct((batch, ln, dilation * LANES), F32)]
    kernel = functools.partial(_dilated_kernel, n_heads=n_heads, dilation=dilation,
                               has_state=state is not None, final=final, qb=qb)
    outs = pl.pallas_call(
        kernel, grid=grid, in_specs=in_specs, out_specs=out_specs, out_shape=out_shape,
        compiler_params=_cparams("parallel", "parallel", "arbitrary"),
        name=f"dilated_d{dilation}",
    )(*args)
    if final:
        return outs[0].reshape(m, hw)
    return outs[0].reshape(m, hw), outs[1].reshape(m, LANES)


def dilated_mixture(y, batch, seq, n_heads):
    state = None
    branches = sorted(DILATED, key=lambda wd: -wd[1])
    for idx, (window, dilation) in enumerate(branches):
        assert window // dilation == 128
        state = dilated_branch(y, state, batch, seq, n_heads, dilation, final=idx == len(branches) - 1)
    return state


def _conv_kernel(a_ref, g_ref, ap_ref, gp_ref, w_ref, cb_ref, lg_ref, lb_ref, o_ref, h_scr, *, tt, taps):
    i = pl.program_id(1)
    halo = h_scr.shape[0] - tt

    def glu(a, g):
        a = a.astype(F32)
        g = g.astype(F32)
        return a * (1.0 / (1.0 + jnp.exp(-g)))

    h_prev = glu(ap_ref[...], gp_ref[...])
    h_scr[0:halo, :] = jnp.where(i > 0, h_prev, 0.0)
    h_scr[halo:, :] = glu(a_ref[...], g_ref[...])
    off = halo - (taps - 1)
    rows = 8
    n_ch = h_scr.shape[1]

    def chunk(c, carry):
        r0 = pl.multiple_of(c * rows, rows)
        win = h_scr[pl.ds(r0, rows + halo), :]
        acc = jnp.zeros((rows, n_ch), F32)
        for j in range(taps):
            acc = acc + win[off + j:off + j + rows, :] * w_ref[j:j + 1, :]
        y = acc + cb_ref[...]
        mu = jnp.mean(y, axis=-1, keepdims=True)
        yc = y - mu
        var = jnp.mean(yc * yc, axis=-1, keepdims=True)
        z = yc * lax.rsqrt(var + EPS) * lg_ref[...] + lb_ref[...]
        o_ref[pl.ds(r0, rows), :] = _silu(z).astype(BF16)
        return carry

    lax.fori_loop(0, tt // rows, chunk, 0)


def conformer_conv(y, conv_w, conv_b, ln_g, ln_b, batch, seq, a_col0, tt=256):
    m = y.shape[0]
    taps, ch = conv_w.shape
    assert taps - 1 <= CONV_HALO
    nt = seq // tt
    ablk = a_col0 // ch
    per = tt // CONV_HALO

    def prev_map(cb):
        return lambda b, i: (jnp.maximum((b * nt + i) * per - 1, 0), cb)

    kernel = functools.partial(_conv_kernel, tt=tt, taps=taps)
    row = lambda a: a.reshape(1, ch)
    return pl.pallas_call(
        kernel,
        grid=(batch, nt),
        in_specs=[
            pl.BlockSpec((tt, ch), lambda b, i: (b * nt + i, ablk)),
            pl.BlockSpec((tt, ch), lambda b, i: (b * nt + i, ablk + 1)),
            pl.BlockSpec((CONV_HALO, ch), prev_map(ablk)),
            pl.BlockSpec((CONV_HALO, ch), prev_map(ablk + 1)),
            pl.BlockSpec((taps, ch), lambda b, i: (0, 0)),
            pl.BlockSpec((1, ch), lambda b, i: (0, 0)),
            pl.BlockSpec((1, ch), lambda b, i: (0, 0)),
            pl.BlockSpec((1, ch), lambda b, i: (0, 0)),
        ],
        out_specs=pl.BlockSpec((tt, ch), lambda b, i: (b * nt + i, 0)),
        out_shape=jax.ShapeDtypeStruct((m, ch), BF16),
        scratch_shapes=[pltpu.VMEM((tt + CONV_HALO, ch), F32)],
        compiler_params=_cparams("parallel", "arbitrary"),
        name="conformer_conv",
    )(y, y, y, y, conv_w, row(conv_b), row(ln_g), row(ln_b))


def _outproj_kernel(a_ref, b_ref, wa_ref, wb_ref, x_ref, g_ref, o_ref):
    y = _dot(a_ref[...], wa_ref[...]) + _dot(b_ref[...], wb_ref[...])
    o_ref[...] = x_ref[...] + g_ref[...] * y


def out_proj(a, b, w_bf16, x2, ada3, g_idx, seq, tm=512):
    m, d = x2.shape
    ka = a.shape[1]
    kb = b.shape[1]
    assert ka == kb
    tiles_per_batch = seq // tm
    return pl.pallas_call(
        _outproj_kernel,
        grid=(m // tm,),
        in_specs=[
            pl.BlockSpec((tm, ka), lambda i: (i, 0)),
            pl.BlockSpec((tm, kb), lambda i: (i, 0)),
            pl.BlockSpec((ka, d), lambda i: (0, 0)),
            pl.BlockSpec((kb, d), lambda i: (1, 0)),
            pl.BlockSpec((tm, d), lambda i: (i, 0)),
            pl.BlockSpec((None, 1, d), lambda i: (i // tiles_per_batch, 0, g_idx)),
        ],
        out_specs=pl.BlockSpec((tm, d), lambda i: (i, 0)),
        out_shape=jax.ShapeDtypeStruct((m, d), F32),
        compiler_params=_cparams("parallel"),
        name="out_proj",
    )(a, b, w_bf16, w_bf16, x2, ada3)


def _ffn_kernel(x_ref, g_ref, sc_ref, sh_ref, gate_ref, wg_ref, wu_ref, wd_ref, o_ref, h_scr, acc_scr):
    f = pl.program_id(1)

    @pl.when(f == 0)
    def _():
        h_scr[...] = _norm_mod(x_ref[...], g_ref[...], sc_ref[...], sh_ref[...]).astype(BF16)
        acc_scr[...] = jnp.zeros(acc_scr.shape, F32)

    h = h_scr[...]
    a = _silu(_dot(h, wg_ref[...])) * _dot(h, wu_ref[...])
    acc_scr[...] += _dot(a.astype(BF16), wd_ref[...])

    @pl.when(f == pl.num_programs(1) - 1)
    def _():
        o_ref[...] = x_ref[...] + gate_ref[...] * acc_scr[...]


def ffn_dense(x2, norm_g, ada3, sc_idx, sh_idx, g_idx, wg, wu, wd, seq, tm=512, tf=512):
    m, d = x2.shape
    dff = wg.shape[1]
    tiles_per_batch = seq // tm
    mod = lambda k: pl.BlockSpec((None, 1, d), lambda i, f: (i // tiles_per_batch, 0, k))
    return pl.pallas_call(
        _ffn_kernel,
        grid=(m // tm, dff // tf),
        in_specs=[
            pl.BlockSpec((tm, d), lambda i, f: (i, 0)),
            pl.BlockSpec((1, d), lambda i, f: (0, 0)),
            mod(sc_idx), mod(sh_idx), mod(g_idx),
            pl.BlockSpec((d, tf), lambda i, f: (0, f)),
            pl.BlockSpec((d, tf), lambda i, f: (0, f)),
            pl.BlockSpec((tf, d), lambda i, f: (f, 0)),
        ],
        out_specs=pl.BlockSpec((tm, d), lambda i, f: (i, 0)),
        out_shape=jax.ShapeDtypeStruct((m, d), F32),
        scratch_shapes=[pltpu.VMEM((tm, d), BF16), pltpu.VMEM((tm, d), F32)],
        compiler_params=_cparams("parallel", "arbitrary"),
        name="ffn_dense",
    )(x2, norm_g.reshape(1, d), ada3, ada3, ada3, wg, wu, wd)


def _router_kernel(x_ref, g_ref, sc_ref, sh_ref, rw_ref, gates_ref, *, n_experts):
    h = _norm_mod(x_ref[...], g_ref[...], sc_ref[...], sh_ref[...])
    logits = jnp.dot(h, rw_ref[...], preferred_element_type=F32, precision=lax.Precision.HIGHEST)
    lane = lax.broadcasted_iota(jnp.int32, logits.shape, 1)
    lg = jnp.where(lane < n_experts, logits, NEG_INF)
    m1 = jnp.max(lg, axis=-1, keepdims=True)
    i1 = jnp.min(jnp.where(lg == m1, lane, LANES), axis=-1, keepdims=True)
    sel1 = lane == i1
    lg2 = jnp.where(sel1, NEG_INF, lg)
    m2 = jnp.max(lg2, axis=-1, keepdims=True)
    i2 = jnp.min(jnp.where(lg2 == m2, lane, LANES), axis=-1, keepdims=True)
    sel2 = lane == i2
    e = jnp.exp(m2 - m1)
    w1 = 1.0 / (1.0 + e)
    w2 = e / (1.0 + e)
    gates_ref[...] = jnp.where(sel1, w1, jnp.where(sel2, w2, 0.0))


def moe_router(x2, norm_g, ada3, sc_idx, sh_idx, router_w, seq, tm=512):
    m, d = x2.shape
    n_experts = router_w.shape[1]
    rw = jnp.zeros((d, LANES), F32).at[:, :n_experts].set(router_w)
    tiles_per_batch = seq // tm
    mod = lambda k: pl.BlockSpec((None, 1, d), lambda i: (i // tiles_per_batch, 0, k))
    return pl.pallas_call(
        functools.partial(_router_kernel, n_experts=n_experts),
        grid=(m // tm,),
        in_specs=[
            pl.BlockSpec((tm, d), lambda i: (i, 0)),
            pl.BlockSpec((1, d), lambda i: (0, 0)),
            mod(sc_idx), mod(sh_idx),
            pl.BlockSpec((d, LANES), lambda i: (0, 0)),
        ],
        out_specs=pl.BlockSpec((tm, LANES), lambda i: (i, 0)),
        out_shape=jax.ShapeDtypeStruct((m, LANES), F32),
        compiler_params=_cparams("parallel"),
        name="moe_router",
    )(x2, norm_g.reshape(1, d), ada3, ada3, rw)


def _moe_dense_kernel(x_ref, g_ref, sc_ref, sh_ref, gate_ref, gates_ref, wg_ref, wu_ref, wd_ref, o_ref,
                      h_scr, acc_scr):
    e = pl.program_id(1)
    f = pl.program_id(2)
    first = jnp.logical_and(e == 0, f == 0)
    last = jnp.logical_and(e == pl.num_programs(1) - 1, f == pl.num_programs(2) - 1)

    @pl.when(first)
    def _():
        h_scr[...] = _norm_mod(x_ref[...], g_ref[...], sc_ref[...], sh_ref[...]).astype(BF16)
        acc_scr[...] = jnp.zeros(acc_scr.shape, F32)

    gates = gates_ref[...]
    lane = lax.broadcasted_iota(jnp.int32, gates.shape, 1)
    ge = jnp.sum(jnp.where(lane == e, gates, 0.0), axis=-1, keepdims=True)
    h = h_scr[...]
    a = _silu(_dot(h, wg_ref[...])) * _dot(h, wu_ref[...]) * ge
    acc_scr[...] += _dot(a.astype(BF16), wd_ref[...])

    @pl.when(last)
    def _():
        o_ref[...] = x_ref[...] + gate_ref[...] * acc_scr[...]


def moe_dense(x2, norm_g, ada3, sc_idx, sh_idx, g_idx, gates, wg, wu, wd, seq, tm=512, tf=512):
    m, d = x2.shape
    n_experts, _, dff = wg.shape
    tiles_per_batch = seq // tm
    mod = lambda k: pl.BlockSpec((None, 1, d), lambda i, e, f: (i // tiles_per_batch, 0, k))
    return pl.pallas_call(
        _moe_dense_kernel,
        grid=(m // tm, n_experts, dff // tf),
        in_specs=[
            pl.BlockSpec((tm, d), lambda i, e, f: (i, 0)),
            pl.BlockSpec((1, d), lambda i, e, f: (0, 0)),
            mod(sc_idx), mod(sh_idx), mod(g_idx),
            pl.BlockSpec((tm, LANES), lambda i, e, f: (i, 0)),
            pl.BlockSpec((None, d, tf), lambda i, e, f: (e, 0, f)),
            pl.BlockSpec((None, d, tf), lambda i, e, f: (e, 0, f)),
            pl.BlockSpec((None, tf, d), lambda i, e, f: (e, f, 0)),
        ],
        out_specs=pl.BlockSpec((tm, d), lambda i, e, f: (i, 0)),
        out_shape=jax.ShapeDtypeStruct((m, d), F32),
        scratch_shapes=[pltpu.VMEM((tm, d), BF16), pltpu.VMEM((tm, d), F32)],
        compiler_params=_cparams("parallel", "arbitrary", "arbitrary"),
        name="moe_dense",
    )(x2, norm_g.reshape(1, d), ada3, ada3, ada3, gates, wg, wu, wd)


def moe_ffn(x2, norm_g, ada3, sc_idx, sh_idx, g_idx, router_w, wg, wu, wd, seq):
    gates = moe_router(x2, norm_g, ada3, sc_idx, sh_idx, router_w, seq)
    return moe_dense(x2, norm_g, ada3, sc_idx, sh_idx, g_idx, gates, wg, wu, wd, seq)


def _head_gain(q_g, k_g, n_q_heads, n_k_heads, q_scale):
    return jnp.concatenate([jnp.tile(q_g * q_scale, n_q_heads), jnp.tile(k_g, n_k_heads)]).reshape(1, -1).astype(F32)


def kernel(x, c, ada_w, ada_b, norm_mix_g, norm_ffn_g, ev_w_in, ev_w_out, a_q_norm_g, a_k_norm_g, a_lambda,
           a_head_norm_g, ffn_w_gate, ffn_w_up, ffn_w_down, od_w_in, od_w_out, c_q_norm_g, c_k_norm_g,
           d_conv_w, d_conv_b, d_ln_g, d_ln_b, moe_router_w, moe_w_gate, moe_w_up, moe_w_down):
    batch, seq, d = x.shape
    depth = ada_w.shape[0]
    m = batch * seq
    scale = HEAD_DIM ** -0.5
    a_width = ev_w_out.shape[1] // 2
    a_heads = a_width // (2 * HEAD_DIM)
    b_heads = a_width // HEAD_DIM
    c_width = od_w_out.shape[1] // 2
    c_heads = c_width // HEAD_DIM

    ada = ada_all(c, ada_w, ada_b)
    x2 = x.reshape(m, d)
    for i in range(depth):
        j = i // 2
        ada3 = ada[i].reshape(batch, 1, 6 * d)
        if i % 2 == 0:
            lam_init = 0.8 - 0.6 * math.exp(-0.3 * i)
            lv = a_lambda[j].astype(F32)
            lam = jnp.exp(jnp.sum(lv[0] * lv[1])) - jnp.exp(jnp.sum(lv[2] * lv[3])) + lam_init
            slopes = 2.0 ** (-8.0 * jnp.arange(1, a_heads + 1, dtype=F32) / a_heads)
            gain = _head_gain(a_q_norm_g[j], a_k_norm_g[j], 2 * a_heads, 2 * a_heads, scale)
            y = in_proj(x2, norm_mix_g[i], ada3, 1, 0, ev_w_in[j].astype(BF16), gain, seq)
            oa = diff_attention(y, slopes, lam.reshape(1), a_head_norm_g[j], batch, seq, a_heads,
                                k_col0=a_width, v_col0=2 * a_width, out_scale=1.0 - lam_init)
            ob = stick_attention(y, batch, seq, b_heads, q_col0=3 * a_width, k_col0=4 * a_width,
                                 v_col0=5 * a_width)
            x2 = out_proj(oa, ob, ev_w_out[j].astype(BF16), x2, ada3, 2, seq)
            x2 = ffn_dense(x2, norm_ffn_g[i], ada3, 4, 3, 5, ffn_w_gate[j].astype(BF16),
                           ffn_w_up[j].astype(BF16), ffn_w_down[j].astype(BF16), seq)
        else:
            gain = _head_gain(c_q_norm_g[j], c_k_norm_g[j], c_heads, c_heads, scale)
            y = in_proj(x2, norm_mix_g[i], ada3, 1, 0, od_w_in[j].astype(BF16), gain, seq)
            oc = dilated_mixture(y, batch, seq, c_heads)
            od = conformer_conv(y, d_conv_w[j], d_conv_b[j], d_ln_g[j], d_ln_b[j], batch, seq,
                                a_col0=3 * c_width)
            x2 = out_proj(oc, od, od_w_out[j].astype(BF16), x2, ada3, 2, seq)
            x2 = moe_ffn(x2, norm_ffn_g[i], ada3, 4, 3, 5, moe_router_w[j], moe_w_gate[j].astype(BF16),
                         moe_w_up[j].astype(BF16), moe_w_down[j].astype(BF16), seq)
    return x2.reshape(batch, seq, d)
```

```python
import functools
import math

import jax
import jax.numpy as jnp
from jax import lax
from jax.experimental import pallas as pl
from jax.experimental.pallas import tpu as pltpu

F32 = jnp.float32
BF16 = jnp.bfloat16
U32 = jnp.uint32
I32 = jnp.int32
EPS = 1e-6
HEAD_DIM = 128
LANES = 128
SUBLANES = 8
VMEM_LIMIT_BYTES = 56 * 1024 * 1024
CONV_HALO = 32
STICK_UNDERFLOW = -104.0
DILATED = ((128, 1), (512, 4), (2048, 16))
NEG_INF = float("-inf")
LOG2E = math.log2(math.e)
MOE_ROW_TILE = 1024


def _cparams(*sem):
    return pltpu.CompilerParams(dimension_semantics=sem, vmem_limit_bytes=VMEM_LIMIT_BYTES)


def _dot(a, b):
    return jnp.dot(a, b, preferred_element_type=F32)


def _dot_nt(a, b):
    return lax.dot_general(a, b, (((1,), (1,)), ((), ())), preferred_element_type=F32)


def _silu(x):
    return x * (1.0 / (1.0 + jnp.exp(-x)))


def _norm_mod(x, g, sc, sh):
    ms = jnp.mean(x * x, axis=-1, keepdims=True)
    return (x * lax.rsqrt(ms + EPS) * g) * (1.0 + sc) + sh


def _pack_bf16_pair(lo, hi):
    lo_bits = lax.bitcast_convert_type(lo.astype(BF16).astype(F32), U32)
    hi_bits = lax.bitcast_convert_type(hi.astype(BF16).astype(F32), U32)
    return hi_bits | (lo_bits >> 16)


def _unpack_bf16_pair(w):
    lo = lax.bitcast_convert_type(w << 16, F32)
    hi = lax.bitcast_convert_type(w & U32(0xFFFF0000), F32)
    return lo, hi


def _ada_kernel(c_ref, w_ref, b_ref, o_ref):
    cond = _silu(c_ref[...])
    o_ref[...] = jnp.dot(cond, w_ref[...], preferred_element_type=F32,
                         precision=lax.Precision.HIGHEST) + b_ref[...]


def ada_all(c, ada_w, ada_b, tn=1024):
    depth, d, n = ada_w.shape
    b = c.shape[0]
    return pl.pallas_call(
        _ada_kernel,
        grid=(depth, n // tn),
        in_specs=[
            pl.BlockSpec((b, d), lambda i, j: (0, 0)),
            pl.BlockSpec((None, d, tn), lambda i, j: (i, 0, j)),
            pl.BlockSpec((None, 1, tn), lambda i, j: (i, 0, j)),
        ],
        out_specs=pl.BlockSpec((None, b, tn), lambda i, j: (i, 0, j)),
        out_shape=jax.ShapeDtypeStruct((depth, b, n), F32),
        compiler_params=_cparams("parallel", "parallel"),
        name="ada",
    )(c, ada_w, ada_b.reshape(depth, 1, n))


def _inproj_kernel(x_ref, g_ref, sc_ref, sh_ref, w_ref, hg_ref, o_ref, h_scr, *, n_norm_tiles):
    j = pl.program_id(1)

    @pl.when(j == 0)
    def _():
        h_scr[...] = _norm_mod(x_ref[...], g_ref[...], sc_ref[...], sh_ref[...]).astype(BF16)

    y = _dot(h_scr[...], w_ref[...])
    tn = y.shape[1]

    @pl.when(j < n_norm_tiles)
    def _():
        for c in range(tn // HEAD_DIM):
            sl = slice(c * HEAD_DIM, (c + 1) * HEAD_DIM)
            blk = y[:, sl]
            r = lax.rsqrt(jnp.mean(blk * blk, axis=-1, keepdims=True) + EPS)
            o_ref[:, sl] = (blk * r * hg_ref[:, sl]).astype(BF16)

    @pl.when(j >= n_norm_tiles)
    def _():
        o_ref[...] = y.astype(BF16)


def in_proj(x2, norm_g, ada3, sc_idx, sh_idx, w_bf16, head_gain, seq, tm=512, tn=1024):
    m, d = x2.shape
    n = w_bf16.shape[1]
    n_norm_tiles = head_gain.shape[1] // tn
    tiles_per_batch = seq // tm
    return pl.pallas_call(
        functools.partial(_inproj_kernel, n_norm_tiles=n_norm_tiles),
        grid=(m // tm, n // tn),
        in_specs=[
            pl.BlockSpec((tm, d), lambda i, j: (i, 0)),
            pl.BlockSpec((1, d), lambda i, j: (0, 0)),
            pl.BlockSpec((None, 1, d), lambda i, j: (i // tiles_per_batch, 0, sc_idx)),
            pl.BlockSpec((None, 1, d), lambda i, j: (i // tiles_per_batch, 0, sh_idx)),
            pl.BlockSpec((d, tn), lambda i, j: (0, j)),
            pl.BlockSpec((1, tn), lambda i, j: (0, jnp.minimum(j, n_norm_tiles - 1))),
        ],
        out_specs=pl.BlockSpec((tm, tn), lambda i, j: (i, j)),
        out_shape=jax.ShapeDtypeStruct((m, n), BF16),
        scratch_shapes=[pltpu.VMEM((tm, d), BF16)],
        compiler_params=_cparams("parallel", "arbitrary"),
        name="in_proj",
    )(x2, norm_g.reshape(1, d), ada3, ada3, w_bf16, head_gain)


def _diff_attn_kernel(slopes_ref, lam_ref, q_ref, k_ref, v_ref, hg_ref, o_ref, m_scr, l_scr, a_scr,
                      *, tq, out_scale):
    h = pl.program_id(1)
    qi = pl.program_id(2)
    slope2 = slopes_ref[h] * LOG2E
    lam = lam_ref[0]
    q = q_ref[...]
    qs = (q[:, :HEAD_DIM], q[:, HEAD_DIM:])
    col = lax.broadcasted_iota(I32, (1, tq), 1).astype(F32)
    reps = tq // LANES

    m_scr[...] = jnp.full(m_scr.shape, NEG_INF, F32)
    l_scr[...] = jnp.zeros(l_scr.shape, F32)
    a_scr[...] = jnp.zeros(a_scr.shape, F32)

    def block(kb, masked):
        ks = pl.multiple_of(kb * tq, tq)
        kblk = k_ref[pl.ds(ks, tq), :]
        vblk = v_ref[pl.ds(ks, tq), :]
        cbias = slope2 * (col + ((kb - qi) * tq).astype(F32))
        ss = [_dot_nt(qs[a], kblk[:, a * HEAD_DIM:(a + 1) * HEAD_DIM]) + cbias for a in range(2)]
        if masked:
            rc = lax.broadcasted_iota(I32, (tq, tq), 0) - lax.broadcasted_iota(I32, (tq, tq), 1)
            ss = [jnp.where(rc >= 0, s, NEG_INF) for s in ss]
        m_olds = [m_scr[a] for a in range(2)]
        m_news = [jnp.maximum(m_old, jnp.max(s, axis=-1, keepdims=True)) for m_old, s in zip(m_olds, ss)]
        ps = [jnp.exp2(s - jnp.tile(m_new, (1, reps))) for s, m_new in zip(ss, m_news)]
        pvs = [_dot(p.astype(BF16), vblk) for p in ps]
        for a in range(2):
            alpha = jnp.exp2(m_olds[a] - m_news[a])
            l_scr[a] = alpha * l_scr[a] + jnp.sum(ps[a], axis=-1, keepdims=True)
            a_scr[a] = jnp.tile(alpha, (1, 2)) * a_scr[a] + pvs[a]
            m_scr[a] = m_news[a]

    def body(kb, carry):
        block(kb, False)
        return carry

    lax.fori_loop(0, qi, body, 0)
    block(qi, True)

    inv1 = jnp.tile(1.0 / l_scr[0], (1, 2))
    inv2 = jnp.tile(1.0 / l_scr[1], (1, 2))
    o = a_scr[0] * inv1 - lam * (a_scr[1] * inv2)
    r = lax.rsqrt(jnp.mean(o * o, axis=-1, keepdims=True) + EPS)
    o_ref[...] = (o * r * hg_ref[...] * out_scale).astype(BF16)


def diff_attention(y, slopes, lam, head_g, batch, seq, n_heads, k_col0, v_col0, out_scale, tq=512):
    m = y.shape[0]
    hw = 2 * HEAD_DIM
    nq = seq // tq
    kernel = functools.partial(_diff_attn_kernel, tq=tq, out_scale=out_scale)
    return pl.pallas_call(
        kernel,
        grid=(batch, n_heads, nq),
        in_specs=[
            pl.BlockSpec(memory_space=pltpu.SMEM),
            pl.BlockSpec(memory_space=pltpu.SMEM),
            pl.BlockSpec((tq, hw), lambda b, h, qi: (b * nq + qi, h)),
            pl.BlockSpec((seq, hw), lambda b, h, qi: (b, k_col0 // hw + h)),
            pl.BlockSpec((seq, hw), lambda b, h, qi: (b, v_col0 // hw + h)),
            pl.BlockSpec((1, hw), lambda b, h, qi: (0, 0)),
        ],
        out_specs=pl.BlockSpec((tq, hw), lambda b, h, qi: (b * nq + qi, h)),
        out_shape=jax.ShapeDtypeStruct((m, n_heads * hw), BF16),
        scratch_shapes=[
            pltpu.VMEM((2, tq, LANES), F32), pltpu.VMEM((2, tq, LANES), F32), pltpu.VMEM((2, tq, hw), F32),
        ],
        compiler_params=_cparams("parallel", "parallel", "arbitrary"),
        name="diff_attn",
    )(slopes, lam, y, y, y, head_g.reshape(1, hw))


def _stick_kernel(q_ref, k_ref, v_ref, o_ref, *scratch, tq, tk, scale, heads):
    acc_scrs, carry_scrs = scratch[:heads], scratch[heads:]
    qi = pl.program_id(2)
    rc = lax.broadcasted_iota(I32, (tq, tk), 0) - lax.broadcasted_iota(I32, (tq, tk), 1)
    jj = lax.broadcasted_iota(I32, (tk, 2 * tk), 0)
    ss = lax.broadcasted_iota(I32, (tk, 2 * tk), 1)
    suffix = jnp.where(jnp.logical_or(jj > ss, ss >= tk), 1.0, 0.0).astype(BF16)

    for ref in scratch:
        ref[...] = jnp.zeros(ref.shape, F32)
    kb0 = (qi * tq + tq) // tk - 1

    def cond(state):
        kb, live = state
        return jnp.logical_and(kb >= 0, live > STICK_UNDERFLOW)

    def body(state):
        kb, _ = state
        ks = pl.multiple_of(kb * tk, tk)
        past = rc + (qi * tq - kb * tk) > 0
        sls = [slice(h * HEAD_DIM, (h + 1) * HEAD_DIM) for h in range(heads)]
        zs = [_dot_nt((q_ref[:, sl].astype(F32) * scale).astype(BF16), k_ref[pl.ds(ks, tk), sl]) for sl in sls]
        lsigs = [jnp.minimum(z, 0.0) - jnp.log(1.0 + jnp.exp(-jnp.abs(z))) for z in zs]
        l1ms = [jnp.where(past, lsig - z, 0.0) for lsig, z in zip(lsigs, zs)]
        his = [l1m.astype(BF16) for l1m in l1ms]
        los = [(l1m - hi.astype(F32)).astype(BF16) for l1m, hi in zip(l1ms, his)]
        sums = [_dot(hi, suffix) + _dot(lo, suffix) for hi, lo in zip(his, los)]
        carries = [ref[...] for ref in carry_scrs]
        ws = [jnp.where(past, jnp.exp(lsig + carry + sm[:, :tk]), 0.0).astype(BF16)
              for lsig, carry, sm in zip(lsigs, carries, sums)]
        pvs = [_dot(w, v_ref[pl.ds(ks, tk), sl]) for w, sl in zip(ws, sls)]
        live = jnp.full((tq, tk), NEG_INF, F32)
        for h in range(heads):
            acc_scrs[h][...] += pvs[h]
            carry = carries[h] + sums[h][:, tk:]
            carry_scrs[h][...] = carry
            live = jnp.maximum(live, carry)
        return kb - 1, jnp.max(live)

    lax.while_loop(cond, body, (kb0, jnp.float32(0.0)))
    for h in range(heads):
        o_ref[:, h * HEAD_DIM:(h + 1) * HEAD_DIM] = acc_scrs[h][...].astype(BF16)


def stick_attention(y, batch, seq, n_heads, q_col0, k_col0, v_col0, tq=256, tk=128, heads=4):
    m = y.shape[0]
    nq = seq // tq
    bw = heads * HEAD_DIM
    assert tk == LANES
    kernel = functools.partial(_stick_kernel, tq=tq, tk=tk, scale=HEAD_DIM ** -0.5, heads=heads)
    return pl.pallas_call(
        kernel,
        grid=(batch, n_heads // heads, nq),
        in_specs=[
            pl.BlockSpec((tq, bw), lambda b, h, qi: (b * nq + qi, q_col0 // bw + h)),
            pl.BlockSpec((seq, bw), lambda b, h, qi: (b, k_col0 // bw + h)),
            pl.BlockSpec((seq, bw), lambda b, h, qi: (b, v_col0 // bw + h)),
        ],
        out_specs=pl.BlockSpec((tq, bw), lambda b, h, qi: (b * nq + qi, h)),
        out_shape=jax.ShapeDtypeStruct((m, n_heads * HEAD_DIM), BF16),
        scratch_shapes=[pltpu.VMEM((tq, HEAD_DIM), F32)] * heads + [pltpu.VMEM((tq, tk), F32)] * heads,
        compiler_params=_cparams("parallel", "parallel", "arbitrary"),
        name="stick_attn",
    )(y, y, y)


def _dilated_kernel(*refs, n_heads, dilation, has_state, final, qb):
    if has_state:
        q_ref, kc_ref, kp_ref, vc_ref, vp_ref, acc_in_ref, ml_in_ref = refs[:7]
        outs = refs[7:]
    else:
        q_ref, kc_ref, kp_ref, vc_ref, vp_ref = refs[:5]
        outs = refs[5:]
    nb = pl.program_id(2)
    row = lax.broadcasted_iota(I32, (qb, qb), 0)
    col = lax.broadcasted_iota(I32, (qb, qb), 1)
    rc = row - col
    valid_c = rc >= 0
    valid_p = rc <= jnp.where(nb > 0, 0, -2 * qb)
    dist_c = rc.astype(F32)
    dist_p = (rc + qb).astype(F32)
    lane = lax.broadcasted_iota(I32, (qb, LANES), 1)
    ml_out = jnp.zeros((qb, LANES), F32)

    sls = [slice(h * HEAD_DIM, (h + 1) * HEAD_DIM) for h in range(n_heads)]
    slopes = [2.0 ** (-8.0 * (h + 1) / n_heads) * dilation for h in range(n_heads)]
    s_cs = [_dot_nt(q_ref[:, sl], kc_ref[:, sl]) + jnp.where(valid_c, -slope * dist_c, NEG_INF)
            for sl, slope in zip(sls, slopes)]
    s_ps = [_dot_nt(q_ref[:, sl], kp_ref[:, sl]) + jnp.where(valid_p, -slope * dist_p, NEG_INF)
            for sl, slope in zip(sls, slopes)]
    m_news = [jnp.maximum(jnp.max(s_c, axis=-1, keepdims=True), jnp.max(s_p, axis=-1, keepdims=True))
              for s_c, s_p in zip(s_cs, s_ps)]
    if has_state:
        m_ins = [ml_in_ref[:, h:h + 1] for h in range(n_heads)]
        m_news = [jnp.maximum(m, m_in) for m, m_in in zip(m_news, m_ins)]
    p_cs = [jnp.exp(s_c - m_new) for s_c, m_new in zip(s_cs, m_news)]
    p_ps = [jnp.exp(s_p - m_new) for s_p, m_new in zip(s_ps, m_news)]
    accs = [_dot(p_c.astype(BF16), vc_ref[:, sl]) + _dot(p_p.astype(BF16), vp_ref[:, sl])
            for p_c, p_p, sl in zip(p_cs, p_ps, sls)]
    for h in range(n_heads):
        l = jnp.sum(p_cs[h], axis=-1, keepdims=True) + jnp.sum(p_ps[h], axis=-1, keepdims=True)
        acc = accs[h]
        if has_state:
            r_in = jnp.exp(m_ins[h] - m_news[h])
            l = l + ml_in_ref[:, n_heads + h:n_heads + h + 1] * r_in
            acc = acc + acc_in_ref[:, sls[h]] * r_in
        if final:
            outs[0][:, sls[h]] = (acc / l).astype(BF16)
        else:
            outs[0][:, sls[h]] = acc
            ml_out = jnp.where(lane == h, m_news[h], ml_out)
            ml_out = jnp.where(lane == n_heads + h, l, ml_out)
    if not final:
        outs[1][...] = ml_out


def dilated_branch(y, state, batch, seq, n_heads, dilation, final, qb=128):
    m, n = y.shape
    hw = n_heads * HEAD_DIM
    groups = n // hw
    ln = seq // dilation
    nblk = ln // qb
    yv = y.reshape(batch, ln, dilation * n)
    grid = (batch, dilation, nblk)

    def cur(g):
        return pl.BlockSpec((None, qb, hw), lambda b, r, i: (b, i, r * groups + g))

    def prev(g):
        return pl.BlockSpec((None, qb, hw), lambda b, r, i: (b, jnp.maximum(i - 1, 0), r * groups + g))

    in_specs = [cur(0), cur(1), prev(1), cur(2), prev(2)]
    args = [yv, yv, yv, yv, yv]
    acc_spec = pl.BlockSpec((None, qb, hw), lambda b, r, i: (b, i, r))
    ml_spec = pl.BlockSpec((None, qb, LANES), lambda b, r, i: (b, i, r))
    if state is not None:
        acc_in, ml_in = state
        in_specs += [acc_spec, ml_spec]
        args += [acc_in.reshape(batch, ln, dilation * hw), ml_in.reshape(batch, ln, dilation * LANES)]
    if final:
        out_specs = [acc_spec]
        out_shape = [jax.ShapeDtypeStruct((batch, ln, dilation * hw), BF16)]
    else:
        out_specs = [acc_spec, ml_spec]
        out_shape = [jax.ShapeDtypeStruct((batch, ln, dilation * hw), F32),
                     jax.ShapeDtypeStruct((batch, ln, dilation * LANES), F32)]
    kernel = functools.partial(_dilated_kernel, n_heads=n_heads, dilation=dilation,
                               has_state=state is not None, final=final, qb=qb)
    outs = pl.pallas_call(
        kernel, grid=grid, in_specs=in_specs, out_specs=out_specs, out_shape=out_shape,
        compiler_params=_cparams("parallel", "parallel", "arbitrary"),
        name=f"dilated_d{dilation}",
    )(*args)
    if final:
        return outs[0].reshape(m, hw)
    return outs[0].reshape(m, hw), outs[1].reshape(m, LANES)


def dilated_mixture(y, batch, seq, n_heads):
    state = None
    branches = sorted(DILATED, key=lambda wd: -wd[1])
    for idx, (window, dilation) in enumerate(branches):
        assert window // dilation == 128
        state = dilated_branch(y, state, batch, seq, n_heads, dilation, final=idx == len(branches) - 1)
    return state


def _conv_kernel(a_ref, g_ref, ap_ref, gp_ref, w_ref, cb_ref, lg_ref, lb_ref, o_ref, h_scr, y_scr,
                 *, tt, taps, rows):
    i = pl.program_id(1)
    halo = h_scr.shape[0] - tt
    n_ch = h_scr.shape[1]
    off = halo - (taps - 1)

    def glu(a, g):
        return a.astype(F32) * (1.0 / (1.0 + jnp.exp(-g.astype(F32))))

    h_scr[0:halo, :] = jnp.where(i > 0, glu(ap_ref[...], gp_ref[...]), 0.0)
    h_scr[halo:, :] = glu(a_ref[...], g_ref[...])

    def conv_chunk(c, carry):
        r0 = pl.multiple_of(c * rows, rows)
        for strip in range(n_ch // LANES):
            ls = slice(strip * LANES, (strip + 1) * LANES)
            win = h_scr[pl.ds(r0, rows + halo), ls]
            acc = jnp.zeros((rows, LANES), F32)
            for b in range(SUBLANES):
                a_vals = [a for a in range(halo // SUBLANES + 1) if 0 <= SUBLANES * a + b - off < taps]
                if not a_vals:
                    continue
                span = SUBLANES * max(a_vals) + rows
                wb = win[b:b + span, :]
                for a in a_vals:
                    j = SUBLANES * a + b - off
                    acc = acc + wb[SUBLANES * a:SUBLANES * a + rows, :] * w_ref[j:j + 1, ls]
            y_scr[pl.ds(r0, rows), ls] = acc
        return carry

    lax.fori_loop(0, tt // rows, conv_chunk, 0)

    def norm_chunk(c, carry):
        r0 = pl.multiple_of(c * rows, rows)
        y = y_scr[pl.ds(r0, rows), :] + cb_ref[...]
        mu = jnp.mean(y, axis=-1, keepdims=True)
        yc = y - mu
        var = jnp.mean(yc * yc, axis=-1, keepdims=True)
        z = yc * lax.rsqrt(var + EPS) * lg_ref[...] + lb_ref[...]
        o_ref[pl.ds(r0, rows), :] = _silu(z).astype(BF16)
        return carry

    lax.fori_loop(0, tt // rows, norm_chunk, 0)


def conformer_conv(y, conv_w, conv_b, ln_g, ln_b, batch, seq, a_col0, tt=256, rows=64):
    m = y.shape[0]
    taps, ch = conv_w.shape
    assert taps - 1 <= CONV_HALO
    nt = seq // tt
    ablk = a_col0 // ch
    per = tt // CONV_HALO

    def prev_map(cb):
        return lambda b, i: (jnp.maximum((b * nt + i) * per - 1, 0), cb)

    kernel = functools.partial(_conv_kernel, tt=tt, taps=taps, rows=rows)
    row = lambda a: a.reshape(1, ch)
    return pl.pallas_call(
        kernel,
        grid=(batch, nt),
        in_specs=[
            pl.BlockSpec((tt, ch), lambda b, i: (b * nt + i, ablk)),
            pl.BlockSpec((tt, ch), lambda b, i: (b * nt + i, ablk + 1)),
            pl.BlockSpec((CONV_HALO, ch), prev_map(ablk)),
            pl.BlockSpec((CONV_HALO, ch), prev_map(ablk + 1)),
            pl.BlockSpec((taps, ch), lambda b, i: (0, 0)),
            pl.BlockSpec((1, ch), lambda b, i: (0, 0)),
            pl.BlockSpec((1, ch), lambda b, i: (0, 0)),
            pl.BlockSpec((1, ch), lambda b, i: (0, 0)),
        ],
        out_specs=pl.BlockSpec((tt, ch), lambda b, i: (b * nt + i, 0)),
        out_shape=jax.ShapeDtypeStruct((m, ch), BF16),
        scratch_shapes=[pltpu.VMEM((tt + CONV_HALO, ch), F32), pltpu.VMEM((tt, ch), F32)],
        compiler_params=_cparams("parallel", "arbitrary"),
        name="conformer_conv",
    )(y, y, y, y, conv_w, row(conv_b), row(ln_g), row(ln_b))


def _outproj_kernel(a_ref, b_ref, wa_ref, wb_ref, x_ref, g_ref, o_ref):
    y = _dot(a_ref[...], wa_ref[...]) + _dot(b_ref[...], wb_ref[...])
    o_ref[...] = x_ref[...] + g_ref[...] * y


def out_proj(a, b, w_bf16, x2, ada3, g_idx, seq, tm=512):
    m, d = x2.shape
    ka = a.shape[1]
    kb = b.shape[1]
    assert ka == kb
    tiles_per_batch = seq // tm
    return pl.pallas_call(
        _outproj_kernel,
        grid=(m // tm,),
        in_specs=[
            pl.BlockSpec((tm, ka), lambda i: (i, 0)),
            pl.BlockSpec((tm, kb), lambda i: (i, 0)),
            pl.BlockSpec((ka, d), lambda i: (0, 0)),
            pl.BlockSpec((kb, d), lambda i: (1, 0)),
            pl.BlockSpec((tm, d), lambda i: (i, 0)),
            pl.BlockSpec((None, 1, d), lambda i: (i // tiles_per_batch, 0, g_idx)),
        ],
        out_specs=pl.BlockSpec((tm, d), lambda i: (i, 0)),
        out_shape=jax.ShapeDtypeStruct((m, d), F32),
        compiler_params=_cparams("parallel"),
        name="out_proj",
    )(a, b, w_bf16, w_bf16, x2, ada3)


def _ffn_kernel(x_ref, g_ref, sc_ref, sh_ref, gate_ref, wg_ref, wu_ref, wd_ref, o_ref, h_scr, acc_scr):
    f = pl.program_id(1)

    @pl.when(f == 0)
    def _():
        h_scr[...] = _norm_mod(x_ref[...], g_ref[...], sc_ref[...], sh_ref[...]).astype(BF16)
        acc_scr[...] = jnp.zeros(acc_scr.shape, F32)

    h = h_scr[...]
    a = _silu(_dot(h, wg_ref[...])) * _dot(h, wu_ref[...])
    acc_scr[...] += _dot(a.astype(BF16), wd_ref[...])

    @pl.when(f == pl.num_programs(1) - 1)
    def _():
        o_ref[...] = x_ref[...] + gate_ref[...] * acc_scr[...]


def ffn_dense(x2, norm_g, ada3, sc_idx, sh_idx, g_idx, wg, wu, wd, seq, tm=512, tf=512):
    m, d = x2.shape
    dff = wg.shape[1]
    tiles_per_batch = seq // tm
    mod = lambda k: pl.BlockSpec((None, 1, d), lambda i, f: (i // tiles_per_batch, 0, k))
    return pl.pallas_call(
        _ffn_kernel,
        grid=(m // tm, dff // tf),
        in_specs=[
            pl.BlockSpec((tm, d), lambda i, f: (i, 0)),
            pl.BlockSpec((1, d), lambda i, f: (0, 0)),
            mod(sc_idx), mod(sh_idx), mod(g_idx),
            pl.BlockSpec((d, tf), lambda i, f: (0, f)),
            pl.BlockSpec((d, tf), lambda i, f: (0, f)),
            pl.BlockSpec((tf, d), lambda i, f: (f, 0)),
        ],
        out_specs=pl.BlockSpec((tm, d), lambda i, f: (i, 0)),
        out_shape=jax.ShapeDtypeStruct((m, d), F32),
        scratch_shapes=[pltpu.VMEM((tm, d), BF16), pltpu.VMEM((tm, d), F32)],
        compiler_params=_cparams("parallel", "arbitrary"),
        name="ffn_dense",
    )(x2, norm_g.reshape(1, d), ada3, ada3, ada3, wg, wu, wd)


def _router_kernel(x_ref, g_ref, sc_ref, sh_ref, rw_ref, cols_ref, rows_ref, counts_ref, *, n_experts):
    i = pl.program_id(0)

    @pl.when(i == 0)
    def _():
        counts_ref[...] = jnp.zeros(counts_ref.shape, F32)

    h = _norm_mod(x_ref[...], g_ref[...], sc_ref[...], sh_ref[...])
    logits = jnp.dot(h, rw_ref[...], preferred_element_type=F32, precision=lax.Precision.HIGHEST)
    tm = logits.shape[0]
    lane = lax.broadcasted_iota(I32, logits.shape, 1)
    lg = jnp.where(lane < n_experts, logits, NEG_INF)
    m1 = jnp.max(lg, axis=-1, keepdims=True)
    i1 = jnp.min(jnp.where(lg == m1, lane, LANES), axis=-1, keepdims=True)
    sel1 = lane == i1
    lg2 = jnp.where(sel1, NEG_INF, lg)
    m2 = jnp.max(lg2, axis=-1, keepdims=True)
    i2 = jnp.min(jnp.where(lg2 == m2, lane, LANES), axis=-1, keepdims=True)
    sel2 = lane == i2
    e = jnp.exp(m2 - m1)
    w1 = 1.0 / (1.0 + e)
    w2 = e / (1.0 + e)

    onehot = jnp.where(jnp.logical_or(sel1, sel2), 1.0, 0.0)
    tri = jnp.where(lax.broadcasted_iota(I32, (tm, tm), 0) > lax.broadcasted_iota(I32, (tm, tm), 1), 1.0, 0.0)
    rank = counts_ref[0:1, :] + _dot(tri.astype(BF16), onehot.astype(BF16))
    rank1 = jnp.sum(jnp.where(sel1, rank, 0.0), axis=-1, keepdims=True)
    rank2 = jnp.sum(jnp.where(sel2, rank, 0.0), axis=-1, keepdims=True)
    counts_ref[0:1, :] = counts_ref[0:1, :] + jnp.sum(onehot, axis=0, keepdims=True)

    cols = jnp.zeros(logits.shape, F32)
    for k, val in enumerate((rank1, rank2, i1.astype(F32), i2.astype(F32), w1, w2)):
        cols = jnp.where(lane == k, val, cols)
    cols_ref[...] = cols
    rows_ref[...] = jnp.transpose(cols)[0:SUBLANES, :].astype(I32)


def moe_router(x2, norm_g, ada3, sc_idx, sh_idx, router_w, seq, tm=512):
    m, d = x2.shape
    n_experts = router_w.shape[1]
    rw = jnp.zeros((d, LANES), F32).at[:, :n_experts].set(router_w)
    tiles_per_batch = seq // tm
    mod = lambda k: pl.BlockSpec((None, 1, d), lambda i: (i // tiles_per_batch, 0, k))
    return pl.pallas_call(
        functools.partial(_router_kernel, n_experts=n_experts),
        grid=(m // tm,),
        in_specs=[
            pl.BlockSpec((tm, d), lambda i: (i, 0)),
            pl.BlockSpec((1, d), lambda i: (0, 0)),
            mod(sc_idx), mod(sh_idx),
            pl.BlockSpec((d, LANES), lambda i: (0, 0)),
        ],
        out_specs=[
            pl.BlockSpec((tm, LANES), lambda i: (i, 0)),
            pl.BlockSpec((SUBLANES, tm), lambda i: (i, 0)),
            pl.BlockSpec((SUBLANES, LANES), lambda i: (0, 0)),
        ],
        out_shape=[
            jax.ShapeDtypeStruct((m, LANES), F32),
            jax.ShapeDtypeStruct((m // tm * SUBLANES, tm), I32),
            jax.ShapeDtypeStruct((SUBLANES, LANES), F32),
        ],
        compiler_params=_cparams("arbitrary"),
        name="moe_router",
    )(x2, norm_g.reshape(1, d), ada3, ada3, rw)


def _row_copy(src, src_row, dst, dst_row, sem):
    return pltpu.make_async_copy(src.at[pl.ds(src_row, 1)], dst.at[pl.ds(dst_row, 1)], sem)


def _dispatch_kernel(x_ref, g_ref, sc_ref, sh_ref, pos_ref, xs_in_ref, xs_ref, hp_scr, pos_smem, sem_pos, sem_rows):
    del xs_in_ref
    tm, half = hp_scr.shape
    pos_copy = pltpu.make_async_copy(pos_ref, pos_smem, sem_pos)
    pos_copy.start()
    h = _norm_mod(x_ref[...], g_ref[...], sc_ref[...], sh_ref[...])
    hp_scr[...] = _pack_bf16_pair(h[:, :half], h[:, half:])
    pos_copy.wait()

    def issue(t, carry):
        _row_copy(hp_scr, t, xs_ref, pos_smem[0, t], sem_rows).start()
        _row_copy(hp_scr, t, xs_ref, pos_smem[1, t], sem_rows).start()
        return carry

    lax.fori_loop(0, tm, issue, 0)

    def drain(t, carry):
        _row_copy(hp_scr, t, xs_ref, pos_smem[0, t], sem_rows).wait()
        _row_copy(hp_scr, t, xs_ref, pos_smem[1, t], sem_rows).wait()
        return carry

    lax.fori_loop(0, tm, drain, 0)


def moe_dispatch(x2, norm_g, ada3, sc_idx, sh_idx, pos_rows, n_rows, seq, tm=512):
    m, d = x2.shape
    half = d // 2
    tiles_per_batch = seq // tm
    mod = lambda k: pl.BlockSpec((None, 1, d), lambda i: (i // tiles_per_batch, 0, k))
    xs0 = jnp.zeros((n_rows, half), U32)
    return pl.pallas_call(
        _dispatch_kernel,
        grid=(m // tm,),
        in_specs=[
            pl.BlockSpec((tm, d), lambda i: (i, 0)),
            pl.BlockSpec((1, d), lambda i: (0, 0)),
            mod(sc_idx), mod(sh_idx),
            pl.BlockSpec((SUBLANES, tm), lambda i: (i, 0)),
            pl.BlockSpec(memory_space=pl.ANY),
        ],
        out_specs=pl.BlockSpec(memory_space=pl.ANY),
        out_shape=jax.ShapeDtypeStruct((n_rows, half), U32),
        scratch_shapes=[
            pltpu.VMEM((tm, half), U32), pltpu.SMEM((SUBLANES, tm), I32),
            pltpu.SemaphoreType.DMA, pltpu.SemaphoreType.DMA,
        ],
        input_output_aliases={5: 0},
        compiler_params=_cparams("arbitrary"),
        name="moe_dispatch",
    )(x2, norm_g.reshape(1, d), ada3, ada3, pos_rows, xs0)


def _moe_ffn_kernel(te_ref, xs_ref, wg_ref, wu_ref, wd_ref, ys_ref, h_scr, acc_scr):
    del te_ref
    f = pl.program_id(1)
    half = xs_ref.shape[1]

    @pl.when(f == 0)
    def _():
        lo, hi = _unpack_bf16_pair(xs_ref[...])
        h_scr[:, :half] = lo.astype(BF16)
        h_scr[:, half:] = hi.astype(BF16)
        acc_scr[...] = jnp.zeros(acc_scr.shape, F32)

    h = h_scr[...]
    a = _silu(_dot(h, wg_ref[...])) * _dot(h, wu_ref[...])
    acc_scr[...] += _dot(a.astype(BF16), wd_ref[...])

    @pl.when(f == pl.num_programs(1) - 1)
    def _():
        ys_ref[...] = _pack_bf16_pair(acc_scr[:, :half], acc_scr[:, half:])


def moe_expert_ffn(xs, tile_expert, wg, wu, wd, tm, tf=512):
    n_rows, half = xs.shape
    d = 2 * half
    dff = wg.shape[2]
    grid_spec = pltpu.PrefetchScalarGridSpec(
        num_scalar_prefetch=1,
        grid=(n_rows // tm, dff // tf),
        in_specs=[
            pl.BlockSpec((tm, half), lambda g, f, te: (g, 0)),
            pl.BlockSpec((None, d, tf), lambda g, f, te: (te[g], 0, f)),
            pl.BlockSpec((None, d, tf), lambda g, f, te: (te[g], 0, f)),
            pl.BlockSpec((None, tf, d), lambda g, f, te: (te[g], f, 0)),
        ],
        out_specs=pl.BlockSpec((tm, half), lambda g, f, te: (g, 0)),
        scratch_shapes=[pltpu.VMEM((tm, d), BF16), pltpu.VMEM((tm, d), F32)],
    )
    return pl.pallas_call(
        _moe_ffn_kernel,
        grid_spec=grid_spec,
        out_shape=jax.ShapeDtypeStruct((n_rows, half), U32),
        compiler_params=_cparams("parallel", "arbitrary"),
        name="moe_expert_ffn",
    )(tile_expert, xs, wg, wu, wd)


def _combine_kernel(x_ref, gate_ref, cols_ref, pos_ref, ys_ref, o_ref, r1_scr, r2_scr, pos_smem, sem_pos, sem_rows):
    tm, half = r1_scr.shape
    pos_copy = pltpu.make_async_copy(pos_ref, pos_smem, sem_pos)
    pos_copy.start()
    pos_copy.wait()

    def issue(t, carry):
        _row_copy(ys_ref, pos_smem[0, t], r1_scr, t, sem_rows).start()
        _row_copy(ys_ref, pos_smem[1, t], r2_scr, t, sem_rows).start()
        return carry

    lax.fori_loop(0, tm, issue, 0)

    def drain(t, carry):
        _row_copy(ys_ref, pos_smem[0, t], r1_scr, t, sem_rows).wait()
        _row_copy(ys_ref, pos_smem[1, t], r2_scr, t, sem_rows).wait()
        return carry

    lax.fori_loop(0, tm, drain, 0)

    w1 = cols_ref[:, 4:5]
    w2 = cols_ref[:, 5:6]
    lo1, hi1 = _unpack_bf16_pair(r1_scr[...])
    lo2, hi2 = _unpack_bf16_pair(r2_scr[...])
    o_ref[:, :half] = x_ref[:, :half] + gate_ref[:, :half] * (w1 * lo1 + w2 * lo2)
    o_ref[:, half:] = x_ref[:, half:] + gate_ref[:, half:] * (w1 * hi1 + w2 * hi2)


def moe_combine(x2, ada3, g_idx, cols, pos_rows, ys, seq, tm=512):
    m, d = x2.shape
    half = d // 2
    tiles_per_batch = seq // tm
    return pl.pallas_call(
        _combine_kernel,
        grid=(m // tm,),
        in_specs=[
            pl.BlockSpec((tm, d), lambda i: (i, 0)),
            pl.BlockSpec((None, 1, d), lambda i: (i // tiles_per_batch, 0, g_idx)),
            pl.BlockSpec((tm, LANES), lambda i: (i, 0)),
            pl.BlockSpec((SUBLANES, tm), lambda i: (i, 0)),
            pl.BlockSpec(memory_space=pl.ANY),
        ],
        out_specs=pl.BlockSpec((tm, d), lambda i: (i, 0)),
        out_shape=jax.ShapeDtypeStruct((m, d), F32),
        scratch_shapes=[
            pltpu.VMEM((tm, half), U32), pltpu.VMEM((tm, half), U32), pltpu.SMEM((SUBLANES, tm), I32),
            pltpu.SemaphoreType.DMA, pltpu.SemaphoreType.DMA,
        ],
        compiler_params=_cparams("arbitrary"),
        name="moe_combine",
    )(x2, ada3, cols, pos_rows, ys)


def moe_ffn(x2, norm_g, ada3, sc_idx, sh_idx, g_idx, router_w, wg, wu, wd, seq, tm=512, row_tile=MOE_ROW_TILE):
    m = x2.shape[0]
    n_experts = router_w.shape[1]
    row_tile = min(row_tile, m)
    cols, rows, counts = moe_router(x2, norm_g, ada3, sc_idx, sh_idx, router_w, seq, tm=tm)
    n_rows = 2 * m + n_experts * row_tile
    counts = counts[0, :n_experts].astype(I32)
    padded = ((counts + row_tile - 1) // row_tile) * row_tile
    ends = jnp.cumsum(padded)
    offs = ends - padded
    rows3 = rows.reshape(m // tm, SUBLANES, tm)
    pos_rows = rows3.at[:, 0].add(offs[rows3[:, 2]]).at[:, 1].add(offs[rows3[:, 3]]).reshape(rows.shape)
    tile_start = jnp.arange(n_rows // row_tile, dtype=I32) * row_tile
    tile_expert = jnp.minimum(jnp.sum(tile_start[:, None] >= ends[None, :], axis=1), n_experts - 1).astype(I32)

    xs = moe_dispatch(x2, norm_g, ada3, sc_idx, sh_idx, pos_rows, n_rows, seq, tm=tm)
    ys = moe_expert_ffn(xs, tile_expert, wg, wu, wd, row_tile)
    return moe_combine(x2, ada3, g_idx, cols, pos_rows, ys, seq, tm=tm)


def _head_gain(q_g, k_g, n_q_heads, n_k_heads, q_scale):
    return jnp.concatenate([jnp.tile(q_g * q_scale, n_q_heads), jnp.tile(k_g, n_k_heads)]).reshape(1, -1).astype(F32)


def kernel(x, c, ada_w, ada_b, norm_mix_g, norm_ffn_g, ev_w_in, ev_w_out, a_q_norm_g, a_k_norm_g, a_lambda,
           a_head_norm_g, ffn_w_gate, ffn_w_up, ffn_w_down, od_w_in, od_w_out, c_q_norm_g, c_k_norm_g,
           d_conv_w, d_conv_b, d_ln_g, d_ln_b, moe_router_w, moe_w_gate, moe_w_up, moe_w_down):
    batch, seq, d = x.shape
    depth = ada_w.shape[0]
    m = batch * seq
    scale = HEAD_DIM ** -0.5
    a_width = ev_w_out.shape[1] // 2
    a_heads = a_width // (2 * HEAD_DIM)
    b_heads = a_width // HEAD_DIM
    c_width = od_w_out.shape[1] // 2
    c_heads = c_width // HEAD_DIM

    ada = ada_all(c, ada_w, ada_b)
    x2 = x.reshape(m, d)
    for i in range(depth):
        j = i // 2
        ada3 = ada[i].reshape(batch, 1, 6 * d)
        if i % 2 == 0:
            lam_init = 0.8 - 0.6 * math.exp(-0.3 * i)
            lv = a_lambda[j].astype(F32)
            lam = jnp.exp(jnp.sum(lv[0] * lv[1])) - jnp.exp(jnp.sum(lv[2] * lv[3])) + lam_init
            slopes = 2.0 ** (-8.0 * jnp.arange(1, a_heads + 1, dtype=F32) / a_heads)
            gain = _head_gain(a_q_norm_g[j], a_k_norm_g[j], 2 * a_heads, 2 * a_heads, scale * LOG2E)
            y = in_proj(x2, norm_mix_g[i], ada3, 1, 0, ev_w_in[j].astype(BF16), gain, seq)
            oa = diff_attention(y, slopes, lam.reshape(1), a_head_norm_g[j], batch, seq, a_heads,
                                k_col0=a_width, v_col0=2 * a_width, out_scale=1.0 - lam_init)
            ob = stick_attention(y, batch, seq, b_heads, q_col0=3 * a_width, k_col0=4 * a_width,
                                 v_col0=5 * a_width)
            x2 = out_proj(oa, ob, ev_w_out[j].astype(BF16), x2, ada3, 2, seq)
            x2 = ffn_dense(x2, norm_ffn_g[i], ada3, 4, 3, 5, ffn_w_gate[j].astype(BF16),
                           ffn_w_up[j].astype(BF16), ffn_w_down[j].astype(BF16), seq)
        else:
            gain = _head_gain(c_q_norm_g[j], c_k_norm_g[j], c_heads, c_heads, scale)
            y = in_proj(x2, norm_mix_g[i], ada3, 1, 0, od_w_in[j].astype(BF16), gain, seq)
            oc = dilated_mixture(y, batch, seq, c_heads)
            od = conformer_conv(y, d_conv_w[j], d_conv_b[j], d_ln_g[j], d_ln_b[j], batch, seq,
                                a_col0=3 * c_width)
            x2 = out_proj(oc, od, od_w_out[j].astype(BF16), x2, ada3, 2, seq)
            x2 = moe_ffn(x2, norm_ffn_g[i], ada3, 4, 3, 5, moe_router_w[j], moe_w_gate[j].astype(BF16),
                         moe_w_up[j].astype(BF16), moe_w_down[j].astype(BF16), seq)
    return x2.reshape(batch, seq, d)
```

```python
import functools
import math

import jax
import jax.numpy as jnp
from jax import lax
from jax.experimental import pallas as pl
from jax.experimental.pallas import tpu as pltpu

F32 = jnp.float32
BF16 = jnp.bfloat16
U32 = jnp.uint32
I32 = jnp.int32
EPS = 1e-6
HEAD_DIM = 128
LANES = 128
SUBLANES = 8
VMEM_LIMIT_BYTES = 56 * 1024 * 1024
CONV_HALO = 32
STICK_UNDERFLOW = -104.0
DILATED = ((128, 1), (512, 4), (2048, 16))
NEG_INF = float("-inf")
LOG2E = math.log2(math.e)
MOE_ROW_TILE = 1024
ROW_DMA_UNROLL = 8


def _cparams(*sem):
    return pltpu.CompilerParams(dimension_semantics=sem, vmem_limit_bytes=VMEM_LIMIT_BYTES)


def _dot(a, b):
    return jnp.dot(a, b, preferred_element_type=F32)


def _dot_nt(a, b):
    return lax.dot_general(a, b, (((1,), (1,)), ((), ())), preferred_element_type=F32)


def _silu(x):
    return x * (1.0 / (1.0 + jnp.exp(-x)))


def _norm_mod(x, g, sc, sh):
    ms = jnp.mean(x * x, axis=-1, keepdims=True)
    return (x * lax.rsqrt(ms + EPS) * g) * (1.0 + sc) + sh


def _pack_bf16_pair(lo, hi):
    lo_bits = lax.bitcast_convert_type(lo.astype(BF16).astype(F32), U32)
    hi_bits = lax.bitcast_convert_type(hi.astype(BF16).astype(F32), U32)
    return hi_bits | (lo_bits >> 16)


def _unpack_bf16_pair(w):
    lo = lax.bitcast_convert_type(w << 16, F32)
    hi = lax.bitcast_convert_type(w & U32(0xFFFF0000), F32)
    return lo, hi


def _ada_kernel(c_ref, w_ref, b_ref, o_ref):
    cond = _silu(c_ref[...])
    o_ref[...] = jnp.dot(cond, w_ref[...], preferred_element_type=F32,
                         precision=lax.Precision.HIGHEST) + b_ref[...]


def ada_all(c, ada_w, ada_b, tn=1024):
    depth, d, n = ada_w.shape
    b = c.shape[0]
    return pl.pallas_call(
        _ada_kernel,
        grid=(depth, n // tn),
        in_specs=[
            pl.BlockSpec((b, d), lambda i, j: (0, 0)),
            pl.BlockSpec((None, d, tn), lambda i, j: (i, 0, j)),
            pl.BlockSpec((None, 1, tn), lambda i, j: (i, 0, j)),
        ],
        out_specs=pl.BlockSpec((None, b, tn), lambda i, j: (i, 0, j)),
        out_shape=jax.ShapeDtypeStruct((depth, b, n), F32),
        compiler_params=_cparams("parallel", "parallel"),
        name="ada",
    )(c, ada_w, ada_b.reshape(depth, 1, n))


def _inproj_kernel(x_ref, g_ref, sc_ref, sh_ref, w_ref, hg_ref, *rest, n_norm_tiles, n_first_tiles):
    o_refs, h_scr = rest[:-1], rest[-1]
    j = pl.program_id(1)

    @pl.when(j == 0)
    def _():
        h_scr[...] = _norm_mod(x_ref[...], g_ref[...], sc_ref[...], sh_ref[...]).astype(BF16)

    y = _dot(h_scr[...], w_ref[...])
    tn = y.shape[1]

    @pl.when(j < n_norm_tiles)
    def _():
        for c in range(tn // HEAD_DIM):
            sl = slice(c * HEAD_DIM, (c + 1) * HEAD_DIM)
            blk = y[:, sl]
            r = lax.rsqrt(jnp.mean(blk * blk, axis=-1, keepdims=True) + EPS)
            o_refs[0][:, sl] = (blk * r * hg_ref[:, sl]).astype(BF16)

    @pl.when(jnp.logical_and(j >= n_norm_tiles, j < n_first_tiles))
    def _():
        o_refs[0][...] = y.astype(BF16)

    if len(o_refs) > 1:
        @pl.when(j >= n_first_tiles)
        def _():
            o_refs[1][...] = y.astype(BF16)


def in_proj(x2, norm_g, ada3, sc_idx, sh_idx, w_stack, layer, head_gain, seq, n_first=None, tm=512, tn=1024):
    m, d = x2.shape
    n = w_stack.shape[2]
    n_first = n if n_first is None else n_first
    n_norm_tiles = head_gain.shape[1] // tn
    n_first_tiles = n_first // tn
    assert n_norm_tiles <= n_first_tiles
    tiles_per_batch = seq // tm
    out_specs = [pl.BlockSpec((tm, tn), lambda i, j: (i, jnp.minimum(j, n_first_tiles - 1)))]
    out_shape = [jax.ShapeDtypeStruct((m, n_first), BF16)]
    if n_first < n:
        out_specs.append(pl.BlockSpec((tm, tn), lambda i, j: (i, jnp.maximum(j - n_first_tiles, 0))))
        out_shape.append(jax.ShapeDtypeStruct((m, n - n_first), BF16))
    return pl.pallas_call(
        functools.partial(_inproj_kernel, n_norm_tiles=n_norm_tiles, n_first_tiles=n_first_tiles),
        grid=(m // tm, n // tn),
        in_specs=[
            pl.BlockSpec((tm, d), lambda i, j: (i, 0)),
            pl.BlockSpec((1, d), lambda i, j: (0, 0)),
            pl.BlockSpec((None, 1, d), lambda i, j: (i // tiles_per_batch, 0, sc_idx)),
            pl.BlockSpec((None, 1, d), lambda i, j: (i // tiles_per_batch, 0, sh_idx)),
            pl.BlockSpec((None, d, tn), lambda i, j: (layer, 0, j)),
            pl.BlockSpec((1, tn), lambda i, j: (0, jnp.minimum(j, n_norm_tiles - 1))),
        ],
        out_specs=out_specs,
        out_shape=out_shape,
        scratch_shapes=[pltpu.VMEM((tm, d), BF16)],
        compiler_params=_cparams("parallel", "arbitrary"),
        name="in_proj",
    )(x2, norm_g.reshape(1, d), ada3, ada3, w_stack, head_gain)


def _diff_attn_kernel(slopes_ref, lam_ref, q_ref, k_ref, v_ref, hg_ref, o_ref, m_scr, l_scr, a_scr,
                      *, tq, out_scale, heads):
    hg0 = pl.program_id(1) * heads
    qi = pl.program_id(2)
    hw = 2 * HEAD_DIM
    lam = lam_ref[0]
    col = lax.broadcasted_iota(I32, (1, tq), 1).astype(F32)
    reps = tq // LANES
    maps = [(h, a) for h in range(heads) for a in range(2)]
    qsl = [slice(h * hw + a * HEAD_DIM, h * hw + (a + 1) * HEAD_DIM) for h, a in maps]
    vsl = [slice(h * hw, (h + 1) * hw) for h, _ in maps]

    m_scr[...] = jnp.full(m_scr.shape, NEG_INF, F32)
    l_scr[...] = jnp.zeros(l_scr.shape, F32)
    a_scr[...] = jnp.zeros(a_scr.shape, F32)

    def block(kb, masked):
        ks = pl.multiple_of(kb * tq, tq)
        kpos = col + ((kb - qi) * tq).astype(F32)
        cbias = [slopes_ref[hg0 + h] * LOG2E * kpos for h in range(heads)]
        ss = [_dot_nt(q_ref[:, sl], k_ref[pl.ds(ks, tq), sl]) + cbias[h] for sl, (h, _) in zip(qsl, maps)]
        if masked:
            rc = lax.broadcasted_iota(I32, (tq, tq), 0) - lax.broadcasted_iota(I32, (tq, tq), 1)
            ss = [jnp.where(rc >= 0, s, NEG_INF) for s in ss]
        m_olds = [m_scr[i] for i in range(len(maps))]
        m_news = [jnp.maximum(m_old, jnp.max(s, axis=-1, keepdims=True)) for m_old, s in zip(m_olds, ss)]
        ps = [jnp.exp2(s - jnp.tile(m_new, (1, reps))) for s, m_new in zip(ss, m_news)]
        pvs = [_dot(p.astype(BF16), v_ref[pl.ds(ks, tq), sl]) for p, sl in zip(ps, vsl)]
        for i in range(len(maps)):
            alpha = jnp.exp2(m_olds[i] - m_news[i])
            l_scr[i] = alpha * l_scr[i] + jnp.sum(ps[i], axis=-1, keepdims=True)
            a_scr[i] = jnp.tile(alpha, (1, 2)) * a_scr[i] + pvs[i]
            m_scr[i] = m_news[i]

    def body(kb, carry):
        block(kb, False)
        return carry

    lax.fori_loop(0, qi, body, 0)
    block(qi, True)

    for h in range(heads):
        inv1 = jnp.tile(1.0 / l_scr[2 * h], (1, 2))
        inv2 = jnp.tile(1.0 / l_scr[2 * h + 1], (1, 2))
        o = a_scr[2 * h] * inv1 - lam * (a_scr[2 * h + 1] * inv2)
        r = lax.rsqrt(jnp.mean(o * o, axis=-1, keepdims=True) + EPS)
        o_ref[:, h * hw:(h + 1) * hw] = (o * r * hg_ref[...] * out_scale).astype(BF16)


def diff_attention(y, slopes, lam, head_g, batch, seq, n_heads, k_col0, v_col0, out_scale, tq=512, heads=2):
    m = y.shape[0]
    hw = 2 * HEAD_DIM
    bw = heads * hw
    nq = seq // tq
    kernel = functools.partial(_diff_attn_kernel, tq=tq, out_scale=out_scale, heads=heads)
    return pl.pallas_call(
        kernel,
        grid=(batch, n_heads // heads, nq),
        in_specs=[
            pl.BlockSpec(memory_space=pltpu.SMEM),
            pl.BlockSpec(memory_space=pltpu.SMEM),
            pl.BlockSpec((tq, bw), lambda b, h, qi: (b * nq + qi, h)),
            pl.BlockSpec((seq, bw), lambda b, h, qi: (b, k_col0 // bw + h)),
            pl.BlockSpec((seq, bw), lambda b, h, qi: (b, v_col0 // bw + h)),
            pl.BlockSpec((1, hw), lambda b, h, qi: (0, 0)),
        ],
        out_specs=pl.BlockSpec((tq, bw), lambda b, h, qi: (b * nq + qi, h)),
        out_shape=jax.ShapeDtypeStruct((m, n_heads * hw), BF16),
        scratch_shapes=[
            pltpu.VMEM((2 * heads, tq, LANES), F32), pltpu.VMEM((2 * heads, tq, LANES), F32),
            pltpu.VMEM((2 * heads, tq, hw), F32),
        ],
        compiler_params=_cparams("parallel", "parallel", "arbitrary"),
        name="diff_attn",
    )(slopes, lam, y, y, y, head_g.reshape(1, hw))


def _stick_kernel(q_ref, k_ref, v_ref, o_ref, *scratch, tq, tk, scale, heads):
    acc_scrs, carry_scrs = scratch[:heads], scratch[heads:]
    qi = pl.program_id(2)
    rc = lax.broadcasted_iota(I32, (tq, tk), 0) - lax.broadcasted_iota(I32, (tq, tk), 1)
    jj = lax.broadcasted_iota(I32, (tk, 2 * tk), 0)
    ss = lax.broadcasted_iota(I32, (tk, 2 * tk), 1)
    suffix = jnp.where(jnp.logical_or(jj > ss, ss >= tk), 1.0, 0.0).astype(BF16)

    for ref in scratch:
        ref[...] = jnp.zeros(ref.shape, F32)
    kb0 = (qi * tq + tq) // tk - 1

    def cond(state):
        kb, live = state
        return jnp.logical_and(kb >= 0, live > STICK_UNDERFLOW)

    def body(state):
        kb, _ = state
        ks = pl.multiple_of(kb * tk, tk)
        past = rc + (qi * tq - kb * tk) > 0
        sls = [slice(h * HEAD_DIM, (h + 1) * HEAD_DIM) for h in range(heads)]
        zs = [_dot_nt((q_ref[:, sl].astype(F32) * scale).astype(BF16), k_ref[pl.ds(ks, tk), sl]) for sl in sls]
        lsigs = [jnp.minimum(z, 0.0) - jnp.log(1.0 + jnp.exp(-jnp.abs(z))) for z in zs]
        l1ms = [jnp.where(past, lsig - z, 0.0) for lsig, z in zip(lsigs, zs)]
        his = [l1m.astype(BF16) for l1m in l1ms]
        los = [(l1m - hi.astype(F32)).astype(BF16) for l1m, hi in zip(l1ms, his)]
        sums = [_dot(hi, suffix) + _dot(lo, suffix) for hi, lo in zip(his, los)]
        carries = [ref[...] for ref in carry_scrs]
        ws = [jnp.where(past, jnp.exp(lsig + carry + sm[:, :tk]), 0.0).astype(BF16)
              for lsig, carry, sm in zip(lsigs, carries, sums)]
        pvs = [_dot(w, v_ref[pl.ds(ks, tk), sl]) for w, sl in zip(ws, sls)]
        live = jnp.full((tq, tk), NEG_INF, F32)
        for h in range(heads):
            acc_scrs[h][...] += pvs[h]
            carry = carries[h] + sums[h][:, tk:]
            carry_scrs[h][...] = carry
            live = jnp.maximum(live, carry)
        return kb - 1, jnp.max(live)

    lax.while_loop(cond, body, (kb0, jnp.float32(0.0)))
    for h in range(heads):
        o_ref[:, h * HEAD_DIM:(h + 1) * HEAD_DIM] = acc_scrs[h][...].astype(BF16)


def stick_attention(y, batch, seq, n_heads, q_col0, k_col0, v_col0, tq=256, tk=128, heads=4):
    m = y.shape[0]
    nq = seq // tq
    bw = heads * HEAD_DIM
    assert tk == LANES
    kernel = functools.partial(_stick_kernel, tq=tq, tk=tk, scale=HEAD_DIM ** -0.5, heads=heads)
    return pl.pallas_call(
        kernel,
        grid=(batch, n_heads // heads, nq),
        in_specs=[
            pl.BlockSpec((tq, bw), lambda b, h, qi: (b * nq + qi, q_col0 // bw + h)),
            pl.BlockSpec((seq, bw), lambda b, h, qi: (b, k_col0 // bw + h)),
            pl.BlockSpec((seq, bw), lambda b, h, qi: (b, v_col0 // bw + h)),
        ],
        out_specs=pl.BlockSpec((tq, bw), lambda b, h, qi: (b * nq + qi, h)),
        out_shape=jax.ShapeDtypeStruct((m, n_heads * HEAD_DIM), BF16),
        scratch_shapes=[pltpu.VMEM((tq, HEAD_DIM), F32)] * heads + [pltpu.VMEM((tq, tk), F32)] * heads,
        compiler_params=_cparams("parallel", "parallel", "arbitrary"),
        name="stick_attn",
    )(y, y, y)


def _dilated_kernel(*refs, n_heads, dilation, has_state, final, qb):
    if has_state:
        q_ref, kc_ref, kp_ref, vc_ref, vp_ref, acc_in_ref, ml_in_ref = refs[:7]
        outs = refs[7:]
    else:
        q_ref, kc_ref, kp_ref, vc_ref, vp_ref = refs[:5]
        outs = refs[5:]
    nb = pl.program_id(2)
    row = lax.broadcasted_iota(I32, (qb, qb), 0)
    col = lax.broadcasted_iota(I32, (qb, qb), 1)
    rc = row - col
    valid_c = rc >= 0
    valid_p = rc <= jnp.where(nb > 0, 0, -2 * qb)
    dist_c = rc.astype(F32)
    dist_p = (rc + qb).astype(F32)
    lane = lax.broadcasted_iota(I32, (qb, LANES), 1)
    ml_out = jnp.zeros((qb, LANES), F32)

    sls = [slice(h * HEAD_DIM, (h + 1) * HEAD_DIM) for h in range(n_heads)]
    slopes = [2.0 ** (-8.0 * (h + 1) / n_heads) * dilation for h in range(n_heads)]
    s_cs = [_dot_nt(q_ref[:, sl], kc_ref[:, sl]) + jnp.where(valid_c, -slope * dist_c, NEG_INF)
            for sl, slope in zip(sls, slopes)]
    s_ps = [_dot_nt(q_ref[:, sl], kp_ref[:, sl]) + jnp.where(valid_p, -slope * dist_p, NEG_INF)
            for sl, slope in zip(sls, slopes)]
    m_news = [jnp.maximum(jnp.max(s_c, axis=-1, keepdims=True), jnp.max(s_p, axis=-1, keepdims=True))
              for s_c, s_p in zip(s_cs, s_ps)]
    if has_state:
        m_ins = [ml_in_ref[:, h:h + 1] for h in range(n_heads)]
        m_news = [jnp.maximum(m, m_in) for m, m_in in zip(m_news, m_ins)]
    p_cs = [jnp.exp(s_c - m_new) for s_c, m_new in zip(s_cs, m_news)]
    p_ps = [jnp.exp(s_p - m_new) for s_p, m_new in zip(s_ps, m_news)]
    accs = [_dot(p_c.astype(BF16), vc_ref[:, sl]) + _dot(p_p.astype(BF16), vp_ref[:, sl])
            for p_c, p_p, sl in zip(p_cs, p_ps, sls)]
    for h in range(n_heads):
        l = jnp.sum(p_cs[h], axis=-1, keepdims=True) + jnp.sum(p_ps[h], axis=-1, keepdims=True)
        acc = accs[h]
        if has_state:
            r_in = jnp.exp(m_ins[h] - m_news[h])
            l = l + ml_in_ref[:, n_heads + h:n_heads + h + 1] * r_in
            acc = acc + acc_in_ref[:, sls[h]] * r_in
        if final:
            outs[0][:, sls[h]] = (acc / l).astype(BF16)
        else:
            outs[0][:, sls[h]] = acc
            ml_out = jnp.where(lane == h, m_news[h], ml_out)
            ml_out = jnp.where(lane == n_heads + h, l, ml_out)
    if not final:
        outs[1][...] = ml_out


def dilated_branch(y, state, batch, seq, n_heads, dilation, final, qb=128):
    m, n = y.shape
    hw = n_heads * HEAD_DIM
    groups = n // hw
    ln = seq // dilation
    nblk = ln // qb
    yv = y.reshape(batch, ln, dilation * n)
    grid = (batch, dilation, nblk)

    def cur(g):
        return pl.BlockSpec((None, qb, hw), lambda b, r, i: (b, i, r * groups + g))

    def prev(g):
        return pl.BlockSpec((None, qb, hw), lambda b, r, i: (b, jnp.maximum(i - 1, 0), r * groups + g))

    in_specs = [cur(0), cur(1), prev(1), cur(2), prev(2)]
    args = [yv, yv, yv, yv, yv]
    acc_spec = pl.BlockSpec((None, qb, hw), lambda b, r, i: (b, i, r))
    ml_spec = pl.BlockSpec((None, qb, LANES), lambda b, r, i: (b, i, r))
    if state is not None:
        acc_in, ml_in = state
        in_specs += [acc_spec, ml_spec]
        args += [acc_in.reshape(batch, ln, dilation * hw), ml_in.reshape(batch, ln, dilation * LANES)]
    if final:
        out_specs = [acc_spec]
        out_shape = [jax.ShapeDtypeStruct((batch, ln, dilation * hw), BF16)]
    else:
        out_specs = [acc_spec, ml_spec]
        out_shape = [jax.ShapeDtypeStruct((batch, ln, dilation * hw), F32),
                     jax.ShapeDtypeStruct((batch, ln, dilation * LANES), F32)]
    kernel = functools.partial(_dilated_kernel, n_heads=n_heads, dilation=dilation,
                               has_state=state is not None, final=final, qb=qb)
    outs = pl.pallas_call(
        kernel, grid=grid, in_specs=in_specs, out_specs=out_specs, out_shape=out_shape,
        compiler_params=_cparams("parallel", "parallel", "arbitrary"),
        name=f"dilated_d{dilation}",
    )(*args)
    if final:
        return outs[0].reshape(m, hw)
    return outs[0].reshape(m, hw), outs[1].reshape(m, LANES)


def dilated_mixture(y, batch, seq, n_heads):
    state = None
    branches = sorted(DILATED, key=lambda wd: -wd[1])
    for idx, (window, dilation) in enumerate(branches):
        assert window // dilation == 128
        state = dilated_branch(y, state, batch, seq, n_heads, dilation, final=idx == len(branches) - 1)
    return state


def _conv_kernel(a_ref, g_ref, ap_ref, gp_ref, w_ref, cb_ref, lg_ref, lb_ref, o_ref, h_scr, y_scr,
                 *, tt, taps, rows):
    i = pl.program_id(1)
    halo = h_scr.shape[0] - tt
    n_ch = h_scr.shape[1]
    off = halo - (taps - 1)

    def glu(a, g):
        return a.astype(F32) * (1.0 / (1.0 + jnp.exp(-g.astype(F32))))

    h_scr[0:halo, :] = jnp.where(i > 0, glu(ap_ref[...], gp_ref[...]), 0.0)
    h_scr[halo:, :] = glu(a_ref[...], g_ref[...])

    def conv_chunk(c, carry):
        r0 = pl.multiple_of(c * rows, rows)
        for strip in range(n_ch // LANES):
            ls = slice(strip * LANES, (strip + 1) * LANES)
            win = h_scr[pl.ds(r0, rows + halo), ls]
            acc = jnp.zeros((rows, LANES), F32)
            for b in range(SUBLANES):
                a_vals = [a for a in range(halo // SUBLANES + 1) if 0 <= SUBLANES * a + b - off < taps]
                if not a_vals:
                    continue
                span = SUBLANES * max(a_vals) + rows
                wb = win[b:b + span, :]
                for a in a_vals:
                    j = SUBLANES * a + b - off
                    acc = acc + wb[SUBLANES * a:SUBLANES * a + rows, :] * w_ref[j:j + 1, ls]
            y_scr[pl.ds(r0, rows), ls] = acc
        return carry

    lax.fori_loop(0, tt // rows, conv_chunk, 0)

    def norm_chunk(c, carry):
        r0 = pl.multiple_of(c * rows, rows)
        y = y_scr[pl.ds(r0, rows), :] + cb_ref[...]
        mu = jnp.mean(y, axis=-1, keepdims=True)
        yc = y - mu
        var = jnp.mean(yc * yc, axis=-1, keepdims=True)
        z = yc * lax.rsqrt(var + EPS) * lg_ref[...] + lb_ref[...]
        o_ref[pl.ds(r0, rows), :] = _silu(z).astype(BF16)
        return carry

    lax.fori_loop(0, tt // rows, norm_chunk, 0)


def conformer_conv(y, conv_w, conv_b, ln_g, ln_b, batch, seq, a_col0, tt=256, rows=64):
    m = y.shape[0]
    taps, ch = conv_w.shape
    assert taps - 1 <= CONV_HALO
    nt = seq // tt
    ablk = a_col0 // ch
    per = tt // CONV_HALO

    def prev_map(cb):
        return lambda b, i: (jnp.maximum((b * nt + i) * per - 1, 0), cb)

    kernel = functools.partial(_conv_kernel, tt=tt, taps=taps, rows=rows)
    row = lambda a: a.reshape(1, ch)
    return pl.pallas_call(
        kernel,
        grid=(batch, nt),
        in_specs=[
            pl.BlockSpec((tt, ch), lambda b, i: (b * nt + i, ablk)),
            pl.BlockSpec((tt, ch), lambda b, i: (b * nt + i, ablk + 1)),
            pl.BlockSpec((CONV_HALO, ch), prev_map(ablk)),
            pl.BlockSpec((CONV_HALO, ch), prev_map(ablk + 1)),
            pl.BlockSpec((taps, ch), lambda b, i: (0, 0)),
            pl.BlockSpec((1, ch), lambda b, i: (0, 0)),
            pl.BlockSpec((1, ch), lambda b, i: (0, 0)),
            pl.BlockSpec((1, ch), lambda b, i: (0, 0)),
        ],
        out_specs=pl.BlockSpec((tt, ch), lambda b, i: (b * nt + i, 0)),
        out_shape=jax.ShapeDtypeStruct((m, ch), BF16),
        scratch_shapes=[pltpu.VMEM((tt + CONV_HALO, ch), F32), pltpu.VMEM((tt, ch), F32)],
        compiler_params=_cparams("parallel", "arbitrary"),
        name="conformer_conv",
    )(y, y, y, y, conv_w, row(conv_b), row(ln_g), row(ln_b))


def _outproj_kernel(a_ref, b_ref, wa_ref, wb_ref, x_ref, g_ref, o_ref):
    y = _dot(a_ref[...], wa_ref[...]) + _dot(b_ref[...], wb_ref[...])
    o_ref[...] = x_ref[...] + g_ref[...] * y


def out_proj(a, b, w_stack, layer, x2, ada3, g_idx, seq, tm=512):
    m, d = x2.shape
    ka = a.shape[1]
    kb = b.shape[1]
    assert ka == kb
    tiles_per_batch = seq // tm
    return pl.pallas_call(
        _outproj_kernel,
        grid=(m // tm,),
        in_specs=[
            pl.BlockSpec((tm, ka), lambda i: (i, 0)),
            pl.BlockSpec((tm, kb), lambda i: (i, 0)),
            pl.BlockSpec((None, ka, d), lambda i: (layer, 0, 0)),
            pl.BlockSpec((None, kb, d), lambda i: (layer, 1, 0)),
            pl.BlockSpec((tm, d), lambda i: (i, 0)),
            pl.BlockSpec((None, 1, d), lambda i: (i // tiles_per_batch, 0, g_idx)),
        ],
        out_specs=pl.BlockSpec((tm, d), lambda i: (i, 0)),
        out_shape=jax.ShapeDtypeStruct((m, d), F32),
        compiler_params=_cparams("parallel"),
        name="out_proj",
    )(a, b, w_stack, w_stack, x2, ada3)


def _ffn_kernel(x_ref, g_ref, sc_ref, sh_ref, gate_ref, wg_ref, wu_ref, wd_ref, o_ref, h_scr, acc_scr):
    f = pl.program_id(1)

    @pl.when(f == 0)
    def _():
        h_scr[...] = _norm_mod(x_ref[...], g_ref[...], sc_ref[...], sh_ref[...]).astype(BF16)
        acc_scr[...] = jnp.zeros(acc_scr.shape, F32)

    h = h_scr[...]
    a = _silu(_dot(h, wg_ref[...])) * _dot(h, wu_ref[...])
    acc_scr[...] += _dot(a.astype(BF16), wd_ref[...])

    @pl.when(f == pl.num_programs(1) - 1)
    def _():
        o_ref[...] = x_ref[...] + gate_ref[...] * acc_scr[...]


def ffn_dense(x2, norm_g, ada3, sc_idx, sh_idx, g_idx, wg, wu, wd, layer, seq, tm=512, tf=512):
    m, d = x2.shape
    dff = wg.shape[2]
    tiles_per_batch = seq // tm
    mod = lambda k: pl.BlockSpec((None, 1, d), lambda i, f: (i // tiles_per_batch, 0, k))
    return pl.pallas_call(
        _ffn_kernel,
        grid=(m // tm, dff // tf),
        in_specs=[
            pl.BlockSpec((tm, d), lambda i, f: (i, 0)),
            pl.BlockSpec((1, d), lambda i, f: (0, 0)),
            mod(sc_idx), mod(sh_idx), mod(g_idx),
            pl.BlockSpec((None, d, tf), lambda i, f: (layer, 0, f)),
            pl.BlockSpec((None, d, tf), lambda i, f: (layer, 0, f)),
            pl.BlockSpec((None, tf, d), lambda i, f: (layer, f, 0)),
        ],
        out_specs=pl.BlockSpec((tm, d), lambda i, f: (i, 0)),
        out_shape=jax.ShapeDtypeStruct((m, d), F32),
        scratch_shapes=[pltpu.VMEM((tm, d), BF16), pltpu.VMEM((tm, d), F32)],
        compiler_params=_cparams("parallel", "arbitrary"),
        name="ffn_dense",
    )(x2, norm_g.reshape(1, d), ada3, ada3, ada3, wg, wu, wd)


def _router_kernel(x_ref, g_ref, sc_ref, sh_ref, rw_ref, cols_ref, rows_ref, counts_ref, *, n_experts):
    i = pl.program_id(0)

    @pl.when(i == 0)
    def _():
        counts_ref[...] = jnp.zeros(counts_ref.shape, F32)

    h = _norm_mod(x_ref[...], g_ref[...], sc_ref[...], sh_ref[...])
    rw = rw_ref[...]
    h_hi, rw_hi = h.astype(BF16), rw.astype(BF16)
    h_lo, rw_lo = (h - h_hi.astype(F32)).astype(BF16), (rw - rw_hi.astype(F32)).astype(BF16)
    logits = _dot(h_hi, rw_hi) + _dot(h_hi, rw_lo) + _dot(h_lo, rw_hi)
    tm = logits.shape[0]
    lane = lax.broadcasted_iota(I32, logits.shape, 1)
    lg = jnp.where(lane < n_experts, logits, NEG_INF)
    m1 = jnp.max(lg, axis=-1, keepdims=True)
    i1 = jnp.min(jnp.where(lg == m1, lane, LANES), axis=-1, keepdims=True)
    sel1 = lane == i1
    lg2 = jnp.where(sel1, NEG_INF, lg)
    m2 = jnp.max(lg2, axis=-1, keepdims=True)
    i2 = jnp.min(jnp.where(lg2 == m2, lane, LANES), axis=-1, keepdims=True)
    sel2 = lane == i2
    e = jnp.exp(m2 - m1)
    w1 = 1.0 / (1.0 + e)
    w2 = e / (1.0 + e)

    onehot = jnp.where(jnp.logical_or(sel1, sel2), 1.0, 0.0)
    tri = jnp.where(lax.broadcasted_iota(I32, (tm, tm), 0) > lax.broadcasted_iota(I32, (tm, tm), 1), 1.0, 0.0)
    rank = counts_ref[0:1, :] + _dot(tri.astype(BF16), onehot.astype(BF16))
    rank1 = jnp.sum(jnp.where(sel1, rank, 0.0), axis=-1, keepdims=True)
    rank2 = jnp.sum(jnp.where(sel2, rank, 0.0), axis=-1, keepdims=True)
    counts_ref[0:1, :] = counts_ref[0:1, :] + jnp.sum(onehot, axis=0, keepdims=True)

    cols = jnp.zeros(logits.shape, F32)
    for k, val in enumerate((rank1, rank2, i1.astype(F32), i2.astype(F32), w1, w2)):
        cols = jnp.where(lane == k, val, cols)
    cols_ref[...] = cols
    rows_ref[...] = jnp.transpose(cols)[0:SUBLANES, :].astype(I32)


def moe_router(x2, norm_g, ada3, sc_idx, sh_idx, router_w, seq, tm=512):
    m, d = x2.shape
    n_experts = router_w.shape[1]
    rw = jnp.zeros((d, LANES), F32).at[:, :n_experts].set(router_w)
    tiles_per_batch = seq // tm
    mod = lambda k: pl.BlockSpec((None, 1, d), lambda i: (i // tiles_per_batch, 0, k))
    return pl.pallas_call(
        functools.partial(_router_kernel, n_experts=n_experts),
        grid=(m // tm,),
        in_specs=[
            pl.BlockSpec((tm, d), lambda i: (i, 0)),
            pl.BlockSpec((1, d), lambda i: (0, 0)),
            mod(sc_idx), mod(sh_idx),
            pl.BlockSpec((d, LANES), lambda i: (0, 0)),
        ],
        out_specs=[
            pl.BlockSpec((tm, LANES), lambda i: (i, 0)),
            pl.BlockSpec((SUBLANES, tm), lambda i: (i, 0)),
            pl.BlockSpec((SUBLANES, LANES), lambda i: (0, 0)),
        ],
        out_shape=[
            jax.ShapeDtypeStruct((m, LANES), F32),
            jax.ShapeDtypeStruct((m // tm * SUBLANES, tm), I32),
            jax.ShapeDtypeStruct((SUBLANES, LANES), F32),
        ],
        compiler_params=_cparams("arbitrary"),
        name="moe_router",
    )(x2, norm_g.reshape(1, d), ada3, ada3, rw)


def _row_copy(src, src_row, dst, dst_row, sem):
    return pltpu.make_async_copy(src.at[pl.ds(src_row, 1)], dst.at[pl.ds(dst_row, 1)], sem)


def _dispatch_kernel(x_ref, g_ref, sc_ref, sh_ref, pos_ref, xs_in_ref, xs_ref, hp_scr, pos_smem, sem_pos, sem_rows):
    del xs_in_ref
    tm, half = hp_scr.shape
    pos_copy = pltpu.make_async_copy(pos_ref, pos_smem, sem_pos)
    pos_copy.start()
    h = _norm_mod(x_ref[...], g_ref[...], sc_ref[...], sh_ref[...])
    hp_scr[...] = _pack_bf16_pair(h[:, :half], h[:, half:])
    pos_copy.wait()

    def issue(t, carry):
        _row_copy(hp_scr, t, xs_ref, pos_smem[0, t], sem_rows).start()
        _row_copy(hp_scr, t, xs_ref, pos_smem[1, t], sem_rows).start()
        return carry

    lax.fori_loop(0, tm, issue, 0, unroll=ROW_DMA_UNROLL)

    def drain(t, carry):
        _row_copy(hp_scr, t, xs_ref, pos_smem[0, t], sem_rows).wait()
        _row_copy(hp_scr, t, xs_ref, pos_smem[1, t], sem_rows).wait()
        return carry

    lax.fori_loop(0, tm, drain, 0, unroll=ROW_DMA_UNROLL)


def moe_dispatch(x2, norm_g, ada3, sc_idx, sh_idx, pos_rows, n_rows, seq, tm=512):
    m, d = x2.shape
    half = d // 2
    tiles_per_batch = seq // tm
    mod = lambda k: pl.BlockSpec((None, 1, d), lambda i: (i // tiles_per_batch, 0, k))
    xs0 = jnp.zeros((n_rows, half), U32)
    return pl.pallas_call(
        _dispatch_kernel,
        grid=(m // tm,),
        in_specs=[
            pl.BlockSpec((tm, d), lambda i: (i, 0)),
            pl.BlockSpec((1, d), lambda i: (0, 0)),
            mod(sc_idx), mod(sh_idx),
            pl.BlockSpec((SUBLANES, tm), lambda i: (i, 0)),
            pl.BlockSpec(memory_space=pl.ANY),
        ],
        out_specs=pl.BlockSpec(memory_space=pl.ANY),
        out_shape=jax.ShapeDtypeStruct((n_rows, half), U32),
        scratch_shapes=[
            pltpu.VMEM((tm, half), U32), pltpu.SMEM((SUBLANES, tm), I32),
            pltpu.SemaphoreType.DMA, pltpu.SemaphoreType.DMA,
        ],
        input_output_aliases={5: 0},
        compiler_params=_cparams("arbitrary"),
        name="moe_dispatch",
    )(x2, norm_g.reshape(1, d), ada3, ada3, pos_rows, xs0)


def _moe_ffn_kernel(te_ref, tv_ref, xs_ref, wg_ref, wu_ref, wd_ref, ys_ref, h_scr, acc_scr):
    del te_ref
    g = pl.program_id(0)
    f = pl.program_id(1)
    half = xs_ref.shape[1]
    used = tv_ref[g] > 0
    last = f == pl.num_programs(1) - 1

    @pl.when(jnp.logical_and(used, f == 0))
    def _():
        lo, hi = _unpack_bf16_pair(xs_ref[...])
        h_scr[:, :half] = lo.astype(BF16)
        h_scr[:, half:] = hi.astype(BF16)
        acc_scr[...] = jnp.zeros(acc_scr.shape, F32)

    @pl.when(used)
    def _():
        h = h_scr[...]
        a = _silu(_dot(h, wg_ref[...])) * _dot(h, wu_ref[...])
        acc_scr[...] += _dot(a.astype(BF16), wd_ref[...])

    @pl.when(jnp.logical_and(used, last))
    def _():
        ys_ref[...] = _pack_bf16_pair(acc_scr[:, :half], acc_scr[:, half:])

    @pl.when(jnp.logical_and(jnp.logical_not(used), last))
    def _():
        ys_ref[...] = jnp.zeros(ys_ref.shape, U32)


def moe_expert_ffn(xs, tile_expert, tile_used, wg, wu, wd, layer, tm, tf=512):
    n_rows, half = xs.shape
    d = 2 * half
    dff = wg.shape[3]
    grid_spec = pltpu.PrefetchScalarGridSpec(
        num_scalar_prefetch=2,
        grid=(n_rows // tm, dff // tf),
        in_specs=[
            pl.BlockSpec((tm, half), lambda g, f, te, tv: (g, 0)),
            pl.BlockSpec((None, None, d, tf), lambda g, f, te, tv: (layer, te[g], 0, f * tv[g])),
            pl.BlockSpec((None, None, d, tf), lambda g, f, te, tv: (layer, te[g], 0, f * tv[g])),
            pl.BlockSpec((None, None, tf, d), lambda g, f, te, tv: (layer, te[g], f * tv[g], 0)),
        ],
        out_specs=pl.BlockSpec((tm, half), lambda g, f, te, tv: (g, 0)),
        scratch_shapes=[pltpu.VMEM((tm, d), BF16), pltpu.VMEM((tm, d), F32)],
    )
    return pl.pallas_call(
        _moe_ffn_kernel,
        grid_spec=grid_spec,
        out_shape=jax.ShapeDtypeStruct((n_rows, half), U32),
        compiler_params=_cparams("parallel", "arbitrary"),
        name="moe_expert_ffn",
    )(tile_expert, tile_used, xs, wg, wu, wd)


def _combine_kernel(x_ref, gate_ref, cols_ref, pos_ref, ys_ref, o_ref, r1_scr, r2_scr, pos_smem, sem_pos, sem_rows):
    tm, half = r1_scr.shape
    pos_copy = pltpu.make_async_copy(pos_ref, pos_smem, sem_pos)
    pos_copy.start()
    pos_copy.wait()

    def issue(t, carry):
        _row_copy(ys_ref, pos_smem[0, t], r1_scr, t, sem_rows).start()
        _row_copy(ys_ref, pos_smem[1, t], r2_scr, t, sem_rows).start()
        return carry

    lax.fori_loop(0, tm, issue, 0, unroll=ROW_DMA_UNROLL)

    def drain(t, carry):
        _row_copy(ys_ref, pos_smem[0, t], r1_scr, t, sem_rows).wait()
        _row_copy(ys_ref, pos_smem[1, t], r2_scr, t, sem_rows).wait()
        return carry

    lax.fori_loop(0, tm, drain, 0, unroll=ROW_DMA_UNROLL)

    w1 = cols_ref[:, 4:5]
    w2 = cols_ref[:, 5:6]
    lo1, hi1 = _unpack_bf16_pair(r1_scr[...])
    lo2, hi2 = _unpack_bf16_pair(r2_scr[...])
    o_ref[:, :half] = x_ref[:, :half] + gate_ref[:, :half] * (w1 * lo1 + w2 * lo2)
    o_ref[:, half:] = x_ref[:, half:] + gate_ref[:, half:] * (w1 * hi1 + w2 * hi2)


def moe_combine(x2, ada3, g_idx, cols, pos_rows, ys, seq, tm=512):
    m, d = x2.shape
    half = d // 2
    tiles_per_batch = seq // tm
    return pl.pallas_call(
        _combine_kernel,
        grid=(m // tm,),
        in_specs=[
            pl.BlockSpec((tm, d), lambda i: (i, 0)),
            pl.BlockSpec((None, 1, d), lambda i: (i // tiles_per_batch, 0, g_idx)),
            pl.BlockSpec((tm, LANES), lambda i: (i, 0)),
            pl.BlockSpec((SUBLANES, tm), lambda i: (i, 0)),
            pl.BlockSpec(memory_space=pl.ANY),
        ],
        out_specs=pl.BlockSpec((tm, d), lambda i: (i, 0)),
        out_shape=jax.ShapeDtypeStruct((m, d), F32),
        scratch_shapes=[
            pltpu.VMEM((tm, half), U32), pltpu.VMEM((tm, half), U32), pltpu.SMEM((SUBLANES, tm), I32),
            pltpu.SemaphoreType.DMA, pltpu.SemaphoreType.DMA,
        ],
        compiler_params=_cparams("arbitrary"),
        name="moe_combine",
    )(x2, ada3, cols, pos_rows, ys)


def moe_ffn(x2, norm_g, ada3, sc_idx, sh_idx, g_idx, router_w, wg, wu, wd, layer, seq, tm=512,
            row_tile=MOE_ROW_TILE):
    m = x2.shape[0]
    n_experts = router_w.shape[1]
    row_tile = min(row_tile, m)
    cols, rows, counts = moe_router(x2, norm_g, ada3, sc_idx, sh_idx, router_w, seq, tm=tm)
    n_rows = 2 * m + n_experts * row_tile
    counts = counts[0, :n_experts].astype(I32)
    padded = ((counts + row_tile - 1) // row_tile) * row_tile
    ends = jnp.cumsum(padded)
    offs = ends - padded
    rows3 = rows.reshape(m // tm, SUBLANES, tm)
    pos_rows = rows3.at[:, 0].add(offs[rows3[:, 2]]).at[:, 1].add(offs[rows3[:, 3]]).reshape(rows.shape)
    tile_start = jnp.arange(n_rows // row_tile, dtype=I32) * row_tile
    tile_expert = jnp.minimum(jnp.sum(tile_start[:, None] >= ends[None, :], axis=1), n_experts - 1).astype(I32)
    tile_used = (tile_start < ends[-1]).astype(I32)

    xs = moe_dispatch(x2, norm_g, ada3, sc_idx, sh_idx, pos_rows, n_rows, seq, tm=tm)
    ys = moe_expert_ffn(xs, tile_expert, tile_used, wg, wu, wd, layer, row_tile)
    return moe_combine(x2, ada3, g_idx, cols, pos_rows, ys, seq, tm=tm)


def _head_gain(q_g, k_g, n_q_heads, n_k_heads, q_scale):
    return jnp.concatenate([jnp.tile(q_g * q_scale, n_q_heads), jnp.tile(k_g, n_k_heads)]).reshape(1, -1).astype(F32)


def kernel(x, c, ada_w, ada_b, norm_mix_g, norm_ffn_g, ev_w_in, ev_w_out, a_q_norm_g, a_k_norm_g, a_lambda,
           a_head_norm_g, ffn_w_gate, ffn_w_up, ffn_w_down, od_w_in, od_w_out, c_q_norm_g, c_k_norm_g,
           d_conv_w, d_conv_b, d_ln_g, d_ln_b, moe_router_w, moe_w_gate, moe_w_up, moe_w_down):
    batch, seq, d = x.shape
    depth = ada_w.shape[0]
    m = batch * seq
    scale = HEAD_DIM ** -0.5
    a_width = ev_w_out.shape[1] // 2
    a_heads = a_width // (2 * HEAD_DIM)
    b_heads = a_width // HEAD_DIM
    c_width = od_w_out.shape[1] // 2
    c_heads = c_width // HEAD_DIM

    ev_w_in, ev_w_out, od_w_in, od_w_out = (w.astype(BF16) for w in (ev_w_in, ev_w_out, od_w_in, od_w_out))
    ffn_w = tuple(w.astype(BF16) for w in (ffn_w_gate, ffn_w_up, ffn_w_down))
    moe_w = tuple(w.astype(BF16) for w in (moe_w_gate, moe_w_up, moe_w_down))

    ada = ada_all(c, ada_w, ada_b)
    x2 = x.reshape(m, d)
    for i in range(depth):
        j = i // 2
        ada3 = ada[i].reshape(batch, 1, 6 * d)
        if i % 2 == 0:
            lam_init = 0.8 - 0.6 * math.exp(-0.3 * i)
            lv = a_lambda[j].astype(F32)
            lam = jnp.exp(jnp.sum(lv[0] * lv[1])) - jnp.exp(jnp.sum(lv[2] * lv[3])) + lam_init
            slopes = 2.0 ** (-8.0 * jnp.arange(1, a_heads + 1, dtype=F32) / a_heads)
            gain = _head_gain(a_q_norm_g[j], a_k_norm_g[j], 2 * a_heads, 2 * a_heads, scale * LOG2E)
            (y,) = in_proj(x2, norm_mix_g[i], ada3, 1, 0, ev_w_in, j, gain, seq)
            oa = diff_attention(y, slopes, lam.reshape(1), a_head_norm_g[j], batch, seq, a_heads,
                                k_col0=a_width, v_col0=2 * a_width, out_scale=1.0 - lam_init)
            ob = stick_attention(y, batch, seq, b_heads, q_col0=3 * a_width, k_col0=4 * a_width,
                                 v_col0=5 * a_width)
            x2 = out_proj(oa, ob, ev_w_out, j, x2, ada3, 2, seq)
            x2 = ffn_dense(x2, norm_ffn_g[i], ada3, 4, 3, 5, *ffn_w, j, seq)
        else:
            gain = _head_gain(c_q_norm_g[j], c_k_norm_g[j], c_heads, c_heads, scale)
            qkv, du = in_proj(x2, norm_mix_g[i], ada3, 1, 0, od_w_in, j, gain, seq, n_first=3 * c_width)
            oc = dilated_mixture(qkv, batch, seq, c_heads)
            od = conformer_conv(du, d_conv_w[j], d_conv_b[j], d_ln_g[j], d_ln_b[j], batch, seq, a_col0=0)
            x2 = out_proj(oc, od, od_w_out, j, x2, ada3, 2, seq)
            x2 = moe_ffn(x2, norm_ffn_g[i], ada3, 4, 3, 5, moe_router_w[j], *moe_w, j, seq)
    return x2.reshape(batch, seq, d)
```

```python
import functools
import math

import jax
import jax.numpy as jnp
from jax import lax
from jax.experimental import pallas as pl
from jax.experimental.pallas import tpu as pltpu

F32 = jnp.float32
BF16 = jnp.bfloat16
U32 = jnp.uint32
I32 = jnp.int32
EPS = 1e-6
HEAD_DIM = 128
LANES = 128
SUBLANES = 8
VMEM_LIMIT_BYTES = 56 * 1024 * 1024
CONV_HALO = 32
STICK_UNDERFLOW = -104.0
DILATED = ((128, 1), (512, 4), (2048, 16))
NEG_INF = float("-inf")
LOG2E = math.log2(math.e)
MOE_ROW_TILE = 1024
ROW_DMA_UNROLL = 8
RESIDUES = 16
PERM_TILE = 512
BAND_STEPS = 128


def _cparams(*sem):
    return pltpu.CompilerParams(dimension_semantics=sem, vmem_limit_bytes=VMEM_LIMIT_BYTES)


def _dot(a, b):
    return jnp.dot(a, b, preferred_element_type=F32)


def _dot_nt(a, b):
    return lax.dot_general(a, b, (((1,), (1,)), ((), ())), preferred_element_type=F32)


def _silu(x):
    return x * (1.0 / (1.0 + jnp.exp(-x)))


def _norm_mod(x, g, sc, sh):
    ms = jnp.mean(x * x, axis=-1, keepdims=True)
    return (x * lax.rsqrt(ms + EPS) * g) * (1.0 + sc) + sh


def _pack_bf16_pair(lo, hi):
    lo_bits = lax.bitcast_convert_type(lo.astype(BF16).astype(F32), U32)
    hi_bits = lax.bitcast_convert_type(hi.astype(BF16).astype(F32), U32)
    return hi_bits | (lo_bits >> 16)


def _unpack_bf16_pair(w):
    lo = lax.bitcast_convert_type(w << 16, F32)
    hi = lax.bitcast_convert_type(w & U32(0xFFFF0000), F32)
    return lo, hi


def _ada_kernel(c_ref, w_ref, b_ref, o_ref):
    cond = _silu(c_ref[...])
    o_ref[...] = jnp.dot(cond, w_ref[...], preferred_element_type=F32,
                         precision=lax.Precision.HIGHEST) + b_ref[...]


def ada_all(c, ada_w, ada_b, tn=1024):
    depth, d, n = ada_w.shape
    b = c.shape[0]
    return pl.pallas_call(
        _ada_kernel,
        grid=(depth, n // tn),
        in_specs=[
            pl.BlockSpec((b, d), lambda i, j: (0, 0)),
            pl.BlockSpec((None, d, tn), lambda i, j: (i, 0, j)),
            pl.BlockSpec((None, 1, tn), lambda i, j: (i, 0, j)),
        ],
        out_specs=pl.BlockSpec((None, b, tn), lambda i, j: (i, 0, j)),
        out_shape=jax.ShapeDtypeStruct((depth, b, n), F32),
        compiler_params=_cparams("parallel", "parallel"),
        name="ada",
    )(c, ada_w, ada_b.reshape(depth, 1, n))


def _residue_perm_matrix(n, inverse):
    c = n // RESIDUES
    i = lax.broadcasted_iota(I32, (n, n), 1 if inverse else 0)
    t = lax.broadcasted_iota(I32, (n, n), 0 if inverse else 1)
    return jnp.where(t == _interleave_pos(i, c, RESIDUES), 1.0, 0.0).astype(BF16)


def _interleave_pos(i, inner, outer):
    shift = inner.bit_length() - 1
    assert inner == 1 << shift
    return (i & (inner - 1)) * outer + (i >> shift)


def _inproj_kernel(x_ref, g_ref, sc_ref, sh_ref, w_ref, hg_ref, *rest, n_out, n_norm_tiles, n_first_tiles,
                   perm_first):
    o_refs, scr = rest[:n_out], rest[n_out:]
    h_scr = scr[0]
    first_lhs = scr[1] if perm_first else h_scr
    j = pl.program_id(1)

    @pl.when(j == 0)
    def _():
        h = _norm_mod(x_ref[...], g_ref[...], sc_ref[...], sh_ref[...]).astype(BF16)
        h_scr[...] = h
        if perm_first:
            first_lhs[...] = _dot(_residue_perm_matrix(h.shape[0], False), h).astype(BF16)

    @pl.when(j < n_first_tiles)
    def _():
        y = _dot(first_lhs[...], w_ref[...])
        tn = y.shape[1]

        @pl.when(j < n_norm_tiles)
        def _():
            for c in range(tn // HEAD_DIM):
                sl = slice(c * HEAD_DIM, (c + 1) * HEAD_DIM)
                blk = y[:, sl]
                r = lax.rsqrt(jnp.mean(blk * blk, axis=-1, keepdims=True) + EPS)
                o_refs[0][:, sl] = (blk * r * hg_ref[:, sl]).astype(BF16)

        @pl.when(j >= n_norm_tiles)
        def _():
            o_refs[0][...] = y.astype(BF16)

    if n_out > 1:
        @pl.when(j >= n_first_tiles)
        def _():
            o_refs[1][...] = _dot(h_scr[...], w_ref[...]).astype(BF16)


def in_proj(x2, norm_g, ada3, sc_idx, sh_idx, w_stack, layer, head_gain, seq, n_first=None, perm_first=False,
            tm=512, tn=1024):
    m, d = x2.shape
    n = w_stack.shape[2]
    n_first = n if n_first is None else n_first
    n_norm_tiles = head_gain.shape[1] // tn
    n_first_tiles = n_first // tn
    assert n_norm_tiles <= n_first_tiles
    assert not perm_first or tm == PERM_TILE
    tiles_per_batch = seq // tm
    out_specs = [pl.BlockSpec((tm, tn), lambda i, j: (i, jnp.minimum(j, n_first_tiles - 1)))]
    out_shape = [jax.ShapeDtypeStruct((m, n_first), BF16)]
    if n_first < n:
        out_specs.append(pl.BlockSpec((tm, tn), lambda i, j: (i, jnp.maximum(j - n_first_tiles, 0))))
        out_shape.append(jax.ShapeDtypeStruct((m, n - n_first), BF16))
    return pl.pallas_call(
        functools.partial(_inproj_kernel, n_out=len(out_specs), n_norm_tiles=n_norm_tiles,
                          n_first_tiles=n_first_tiles, perm_first=perm_first),
        grid=(m // tm, n // tn),
        in_specs=[
            pl.BlockSpec((tm, d), lambda i, j: (i, 0)),
            pl.BlockSpec((1, d), lambda i, j: (0, 0)),
            pl.BlockSpec((None, 1, d), lambda i, j: (i // tiles_per_batch, 0, sc_idx)),
            pl.BlockSpec((None, 1, d), lambda i, j: (i // tiles_per_batch, 0, sh_idx)),
            pl.BlockSpec((None, d, tn), lambda i, j: (layer, 0, j)),
            pl.BlockSpec((1, tn), lambda i, j: (0, jnp.minimum(j, n_norm_tiles - 1))),
        ],
        out_specs=out_specs,
        out_shape=out_shape,
        scratch_shapes=[pltpu.VMEM((tm, d), BF16)] * (2 if perm_first else 1),
        compiler_params=_cparams("parallel", "arbitrary"),
        name="in_proj",
    )(x2, norm_g.reshape(1, d), ada3, ada3, w_stack, head_gain)


def _diff_attn_kernel(slopes_ref, lam_ref, q_ref, k_ref, v_ref, hg_ref, o_ref, m_scr, l_scr, a_scr,
                      *, tq, out_scale, heads):
    hg0 = pl.program_id(1) * heads
    qi = pl.program_id(2)
    hw = 2 * HEAD_DIM
    lam = lam_ref[0]
    col = lax.broadcasted_iota(I32, (1, tq), 1).astype(F32)
    reps = tq // LANES
    maps = [(h, a) for h in range(heads) for a in range(2)]
    qsl = [slice(h * hw + a * HEAD_DIM, h * hw + (a + 1) * HEAD_DIM) for h, a in maps]
    vsl = [slice(h * hw, (h + 1) * hw) for h, _ in maps]

    m_scr[...] = jnp.full(m_scr.shape, NEG_INF, F32)
    l_scr[...] = jnp.zeros(l_scr.shape, F32)
    a_scr[...] = jnp.zeros(a_scr.shape, F32)

    def block(kb, masked):
        ks = pl.multiple_of(kb * tq, tq)
        kpos = col + ((kb - qi) * tq).astype(F32)
        cbias = [slopes_ref[hg0 + h] * LOG2E * kpos for h in range(heads)]
        ss = [_dot_nt(q_ref[:, sl], k_ref[pl.ds(ks, tq), sl]) + cbias[h] for sl, (h, _) in zip(qsl, maps)]
        if masked:
            rc = lax.broadcasted_iota(I32, (tq, tq), 0) - lax.broadcasted_iota(I32, (tq, tq), 1)
            ss = [jnp.where(rc >= 0, s, NEG_INF) for s in ss]
        m_olds = [m_scr[i] for i in range(len(maps))]
        m_news = [jnp.maximum(m_old, jnp.max(s, axis=-1, keepdims=True)) for m_old, s in zip(m_olds, ss)]
        ps = [jnp.exp2(s - jnp.tile(m_new, (1, reps))) for s, m_new in zip(ss, m_news)]
        pvs = [_dot(p.astype(BF16), v_ref[pl.ds(ks, tq), sl]) for p, sl in zip(ps, vsl)]
        for i in range(len(maps)):
            alpha = jnp.exp2(m_olds[i] - m_news[i])
            l_scr[i] = alpha * l_scr[i] + jnp.sum(ps[i], axis=-1, keepdims=True)
            a_scr[i] = jnp.tile(alpha, (1, 2)) * a_scr[i] + pvs[i]
            m_scr[i] = m_news[i]

    def body(kb, carry):
        block(kb, False)
        return carry

    lax.fori_loop(0, qi, body, 0)
    block(qi, True)

    for h in range(heads):
        inv1 = jnp.tile(1.0 / l_scr[2 * h], (1, 2))
        inv2 = jnp.tile(1.0 / l_scr[2 * h + 1], (1, 2))
        o = a_scr[2 * h] * inv1 - lam * (a_scr[2 * h + 1] * inv2)
        r = lax.rsqrt(jnp.mean(o * o, axis=-1, keepdims=True) + EPS)
        o_ref[:, h * hw:(h + 1) * hw] = (o * r * hg_ref[...] * out_scale).astype(BF16)


def diff_attention(y, slopes, lam, head_g, batch, seq, n_heads, k_col0, v_col0, out_scale, tq=512, heads=2):
    m = y.shape[0]
    hw = 2 * HEAD_DIM
    bw = heads * hw
    nq = seq // tq
    kernel = functools.partial(_diff_attn_kernel, tq=tq, out_scale=out_scale, heads=heads)
    return pl.pallas_call(
        kernel,
        grid=(batch, n_heads // heads, nq),
        in_specs=[
            pl.BlockSpec(memory_space=pltpu.SMEM),
            pl.BlockSpec(memory_space=pltpu.SMEM),
            pl.BlockSpec((tq, bw), lambda b, h, qi: (b * nq + qi, h)),
            pl.BlockSpec((seq, bw), lambda b, h, qi: (b, k_col0 // bw + h)),
            pl.BlockSpec((seq, bw), lambda b, h, qi: (b, v_col0 // bw + h)),
            pl.BlockSpec((1, hw), lambda b, h, qi: (0, 0)),
        ],
        out_specs=pl.BlockSpec((tq, bw), lambda b, h, qi: (b * nq + qi, h)),
        out_shape=jax.ShapeDtypeStruct((m, n_heads * hw), BF16),
        scratch_shapes=[
            pltpu.VMEM((2 * heads, tq, LANES), F32), pltpu.VMEM((2 * heads, tq, LANES), F32),
            pltpu.VMEM((2 * heads, tq, hw), F32),
        ],
        compiler_params=_cparams("parallel", "parallel", "arbitrary"),
        name="diff_attn",
    )(slopes, lam, y, y, y, head_g.reshape(1, hw))


def _stick_kernel(q_ref, k_ref, v_ref, o_ref, *scratch, tq, tk, scale, heads):
    acc_scrs, carry_scrs = scratch[:heads], scratch[heads:]
    qi = pl.program_id(2)
    rc = lax.broadcasted_iota(I32, (tq, tk), 0) - lax.broadcasted_iota(I32, (tq, tk), 1)
    jj = lax.broadcasted_iota(I32, (tk, 2 * tk), 0)
    ss = lax.broadcasted_iota(I32, (tk, 2 * tk), 1)
    suffix = jnp.where(jnp.logical_or(jj > ss, ss >= tk), 1.0, 0.0).astype(BF16)

    for ref in scratch:
        ref[...] = jnp.zeros(ref.shape, F32)
    kb0 = (qi * tq + tq) // tk - 1

    def cond(state):
        kb, live = state
        return jnp.logical_and(kb >= 0, live > STICK_UNDERFLOW)

    def body(state):
        kb, _ = state
        ks = pl.multiple_of(kb * tk, tk)
        past = rc + (qi * tq - kb * tk) > 0
        sls = [slice(h * HEAD_DIM, (h + 1) * HEAD_DIM) for h in range(heads)]
        zs = [_dot_nt((q_ref[:, sl].astype(F32) * scale).astype(BF16), k_ref[pl.ds(ks, tk), sl]) for sl in sls]
        lsigs = [jnp.minimum(z, 0.0) - jnp.log(1.0 + jnp.exp(-jnp.abs(z))) for z in zs]
        l1ms = [jnp.where(past, lsig - z, 0.0) for lsig, z in zip(lsigs, zs)]
        his = [l1m.astype(BF16) for l1m in l1ms]
        los = [(l1m - hi.astype(F32)).astype(BF16) for l1m, hi in zip(l1ms, his)]
        sums = [_dot(hi, suffix) + _dot(lo, suffix) for hi, lo in zip(his, los)]
        carries = [ref[...] for ref in carry_scrs]
        ws = [jnp.where(past, jnp.exp(lsig + carry + sm[:, :tk]), 0.0).astype(BF16)
              for lsig, carry, sm in zip(lsigs, carries, sums)]
        pvs = [_dot(w, v_ref[pl.ds(ks, tk), sl]) for w, sl in zip(ws, sls)]
        live = jnp.full((tq, tk), NEG_INF, F32)
        for h in range(heads):
            acc_scrs[h][...] += pvs[h]
            carry = carries[h] + sums[h][:, tk:]
            carry_scrs[h][...] = carry
            live = jnp.maximum(live, carry)
        return kb - 1, jnp.max(live)

    lax.while_loop(cond, body, (kb0, jnp.float32(0.0)))
    for h in range(heads):
        o_ref[:, h * HEAD_DIM:(h + 1) * HEAD_DIM] = acc_scrs[h][...].astype(BF16)


def stick_attention(y, batch, seq, n_heads, q_col0, k_col0, v_col0, tq=256, tk=128, heads=4):
    m = y.shape[0]
    nq = seq // tq
    bw = heads * HEAD_DIM
    assert tk == LANES
    kernel = functools.partial(_stick_kernel, tq=tq, tk=tk, scale=HEAD_DIM ** -0.5, heads=heads)
    return pl.pallas_call(
        kernel,
        grid=(batch, n_heads // heads, nq),
        in_specs=[
            pl.BlockSpec((tq, bw), lambda b, h, qi: (b * nq + qi, q_col0 // bw + h)),
            pl.BlockSpec((seq, bw), lambda b, h, qi: (b, k_col0 // bw + h)),
            pl.BlockSpec((seq, bw), lambda b, h, qi: (b, v_col0 // bw + h)),
        ],
        out_specs=pl.BlockSpec((tq, bw), lambda b, h, qi: (b * nq + qi, h)),
        out_shape=jax.ShapeDtypeStruct((m, n_heads * HEAD_DIM), BF16),
        scratch_shapes=[pltpu.VMEM((tq, HEAD_DIM), F32)] * heads + [pltpu.VMEM((tq, tk), F32)] * heads,
        compiler_params=_cparams("parallel", "parallel", "arbitrary"),
        name="stick_attn",
    )(y, y, y)


def _dilated_kernel(*refs, n_heads, dilation, has_state, final, qb, pos_inner, pos_outer):
    if has_state:
        q_ref, kc_ref, kp_ref, vc_ref, vp_ref, acc_in_ref, ml_in_ref = refs[:7]
        outs = refs[7:]
    else:
        q_ref, kc_ref, kp_ref, vc_ref, vp_ref = refs[:5]
        outs = refs[5:]
    nb = pl.program_id(2)
    blk_shape = q_ref.shape

    def rows2d(ref):
        return ref[...].reshape(qb, ref.shape[-1])

    q, kc, kp, vc, vp = (rows2d(r) for r in (q_ref, kc_ref, kp_ref, vc_ref, vp_ref))
    row = lax.broadcasted_iota(I32, (qb, qb), 0)
    col = lax.broadcasted_iota(I32, (qb, qb), 1)
    if pos_outer > 1:
        row, col = _interleave_pos(row, pos_inner, pos_outer), _interleave_pos(col, pos_inner, pos_outer)
    steps_c = row - col
    steps_p = steps_c + qb
    in_band_c = jnp.abs(steps_c - BAND_STEPS // 2) <= BAND_STEPS // 2
    in_band_p = steps_p <= jnp.where(nb > 0, BAND_STEPS, 0)
    dist_c = steps_c.astype(F32)
    dist_p = steps_p.astype(F32)
    lane = lax.broadcasted_iota(I32, (qb, LANES), 1)
    ml_out = jnp.zeros((qb, LANES), F32)

    sls = [slice(h * HEAD_DIM, (h + 1) * HEAD_DIM) for h in range(n_heads)]
    slopes = [2.0 ** (-8.0 * (h + 1) / n_heads) * dilation for h in range(n_heads)]
    s_cs = [_dot_nt(q[:, sl], kc[:, sl]) + jnp.where(in_band_c, -slope * dist_c, NEG_INF)
            for sl, slope in zip(sls, slopes)]
    s_ps = [_dot_nt(q[:, sl], kp[:, sl]) + jnp.where(in_band_p, -slope * dist_p, NEG_INF)
            for sl, slope in zip(sls, slopes)]
    m_news = [jnp.maximum(jnp.max(s_c, axis=-1, keepdims=True), jnp.max(s_p, axis=-1, keepdims=True))
              for s_c, s_p in zip(s_cs, s_ps)]
    if has_state:
        acc_in = rows2d(acc_in_ref)
        ml_in = rows2d(ml_in_ref)
        m_ins = [ml_in[:, h:h + 1] for h in range(n_heads)]
        m_news = [jnp.maximum(m, m_in) for m, m_in in zip(m_news, m_ins)]
    p_cs = [jnp.exp(s_c - m_new) for s_c, m_new in zip(s_cs, m_news)]
    p_ps = [jnp.exp(s_p - m_new) for s_p, m_new in zip(s_ps, m_news)]
    accs = [_dot(p_c.astype(BF16), vc[:, sl]) + _dot(p_p.astype(BF16), vp[:, sl])
            for p_c, p_p, sl in zip(p_cs, p_ps, sls)]
    out_cols = []
    for h in range(n_heads):
        l = jnp.sum(p_cs[h], axis=-1, keepdims=True) + jnp.sum(p_ps[h], axis=-1, keepdims=True)
        acc = accs[h]
        if has_state:
            r_in = jnp.exp(m_ins[h] - m_news[h])
            l = l + ml_in[:, n_heads + h:n_heads + h + 1] * r_in
            acc = acc + acc_in[:, sls[h]] * r_in
        if final:
            out_cols.append((acc / l).astype(BF16))
        else:
            out_cols.append(acc)
            ml_out = jnp.where(lane == h, m_news[h], ml_out)
            ml_out = jnp.where(lane == n_heads + h, l, ml_out)
    outs[0][...] = jnp.concatenate(out_cols, axis=1).reshape(blk_shape)
    if not final:
        outs[1][...] = ml_out.reshape(outs[1].shape)


def dilated_branch(qkv, state, batch, seq, n_heads, dilation, final):
    m, n = qkv.shape
    hw = n_heads * HEAD_DIM
    tiles = m // PERM_TILE
    tpb = seq // PERM_TILE
    rows = PERM_TILE // RESIDUES
    if dilation == RESIDUES:
        span = BAND_STEPS // rows
        assert tpb % span == 0
        qb, pos = BAND_STEPS, (1, 1)
        grid = (batch, RESIDUES, tpb // span)
        view = lambda a: a.reshape(tiles, RESIDUES, rows, a.shape[-1])
        blk = lambda w: (span, None, rows, w)
        idx = lambda b, r, i, g: (b * (tpb // span) + i, r, 0, g)
    elif dilation * BAND_STEPS == PERM_TILE:
        sub = RESIDUES // dilation
        qb, pos = BAND_STEPS, (rows, sub)
        grid = (batch, dilation, tpb)
        view = lambda a: a.reshape(tiles, sub, dilation, rows, a.shape[-1])
        blk = lambda w: (None, sub, None, rows, w)
        idx = lambda b, r, i, g: (b * tpb + i, 0, r, 0, g)
    else:
        assert dilation == 1
        qb = 2 * BAND_STEPS
        sub = qb // RESIDUES
        per_tile = rows // sub
        qb, pos = qb, (sub, RESIDUES)
        grid = (batch, 1, tpb * per_tile)
        view = lambda a: a.reshape(tiles, RESIDUES, rows, a.shape[-1])
        blk = lambda w: (None, RESIDUES, sub, w)
        idx = lambda b, r, i, g: (b * tpb + i // per_tile, 0, i % per_tile, g)

    def spec(w, g, back):
        return pl.BlockSpec(blk(w), lambda b, r, i: idx(b, r, jnp.maximum(i - back, 0), g))

    qv = view(qkv)
    in_specs = [spec(hw, 0, 0), spec(hw, 1, 0), spec(hw, 1, 1), spec(hw, 2, 0), spec(hw, 2, 1)]
    args = [qv] * 5
    if state is not None:
        in_specs += [spec(hw, 0, 0), spec(LANES, 0, 0)]
        args += [view(state[0]), view(state[1])]
    def shape(w, dt):
        return jax.ShapeDtypeStruct(jax.eval_shape(view, jax.ShapeDtypeStruct((m, w), dt)).shape, dt)

    if final:
        out_specs = [spec(hw, 0, 0)]
        out_shape = [shape(hw, BF16)]
    else:
        out_specs = [spec(hw, 0, 0), spec(LANES, 0, 0)]
        out_shape = [shape(hw, F32), shape(LANES, F32)]
    kernel = functools.partial(_dilated_kernel, n_heads=n_heads, dilation=dilation, has_state=state is not None,
                               final=final, qb=qb, pos_inner=pos[0], pos_outer=pos[1])
    outs = pl.pallas_call(
        kernel, grid=grid, in_specs=in_specs, out_specs=out_specs, out_shape=out_shape,
        compiler_params=_cparams("parallel", "parallel", "arbitrary"),
        name=f"dilated_d{dilation}",
    )(*args)
    if final:
        return outs[0].reshape(m, hw)
    return outs[0].reshape(m, hw), outs[1].reshape(m, LANES)


def dilated_mixture(qkv, batch, seq, n_heads):
    state = None
    branches = sorted(DILATED, key=lambda wd: -wd[1])
    for idx, (window, dilation) in enumerate(branches):
        assert window // dilation == BAND_STEPS
        state = dilated_branch(qkv, state, batch, seq, n_heads, dilation, final=idx == len(branches) - 1)
    return state


def _conv_kernel(a_ref, g_ref, ap_ref, gp_ref, w_ref, cb_ref, lg_ref, lb_ref, o_ref, h_scr, y_scr,
                 *, tt, taps, rows):
    i = pl.program_id(1)
    halo = h_scr.shape[0] - tt
    n_ch = h_scr.shape[1]
    off = halo - (taps - 1)

    def glu(a, g):
        return a.astype(F32) * (1.0 / (1.0 + jnp.exp(-g.astype(F32))))

    h_scr[0:halo, :] = jnp.where(i > 0, glu(ap_ref[...], gp_ref[...]), 0.0)
    h_scr[halo:, :] = glu(a_ref[...], g_ref[...])

    def conv_chunk(c, carry):
        r0 = pl.multiple_of(c * rows, rows)
        for strip in range(n_ch // LANES):
            ls = slice(strip * LANES, (strip + 1) * LANES)
            win = h_scr[pl.ds(r0, rows + halo), ls]
            acc = jnp.zeros((rows, LANES), F32)
            for b in range(SUBLANES):
                a_vals = [a for a in range(halo // SUBLANES + 1) if 0 <= SUBLANES * a + b - off < taps]
                if not a_vals:
                    continue
                span = SUBLANES * max(a_vals) + rows
                wb = win[b:b + span, :]
                for a in a_vals:
                    j = SUBLANES * a + b - off
                    acc = acc + wb[SUBLANES * a:SUBLANES * a + rows, :] * w_ref[j:j + 1, ls]
            y_scr[pl.ds(r0, rows), ls] = acc
        return carry

    lax.fori_loop(0, tt // rows, conv_chunk, 0)

    def norm_chunk(c, carry):
        r0 = pl.multiple_of(c * rows, rows)
        y = y_scr[pl.ds(r0, rows), :] + cb_ref[...]
        mu = jnp.mean(y, axis=-1, keepdims=True)
        yc = y - mu
        var = jnp.mean(yc * yc, axis=-1, keepdims=True)
        z = yc * lax.rsqrt(var + EPS) * lg_ref[...] + lb_ref[...]
        o_ref[pl.ds(r0, rows), :] = _silu(z).astype(BF16)
        return carry

    lax.fori_loop(0, tt // rows, norm_chunk, 0)


def conformer_conv(y, conv_w, conv_b, ln_g, ln_b, batch, seq, a_col0, tt=256, rows=64):
    m = y.shape[0]
    taps, ch = conv_w.shape
    assert taps - 1 <= CONV_HALO
    nt = seq // tt
    ablk = a_col0 // ch
    per = tt // CONV_HALO

    def prev_map(cb):
        return lambda b, i: (jnp.maximum((b * nt + i) * per - 1, 0), cb)

    kernel = functools.partial(_conv_kernel, tt=tt, taps=taps, rows=rows)
    row = lambda a: a.reshape(1, ch)
    return pl.pallas_call(
        kernel,
        grid=(batch, nt),
        in_specs=[
            pl.BlockSpec((tt, ch), lambda b, i: (b * nt + i, ablk)),
            pl.BlockSpec((tt, ch), lambda b, i: (b * nt + i, ablk + 1)),
            pl.BlockSpec((CONV_HALO, ch), prev_map(ablk)),
            pl.BlockSpec((CONV_HALO, ch), prev_map(ablk + 1)),
            pl.BlockSpec((taps, ch), lambda b, i: (0, 0)),
            pl.BlockSpec((1, ch), lambda b, i: (0, 0)),
            pl.BlockSpec((1, ch), lambda b, i: (0, 0)),
            pl.BlockSpec((1, ch), lambda b, i: (0, 0)),
        ],
        out_specs=pl.BlockSpec((tt, ch), lambda b, i: (b * nt + i, 0)),
        out_shape=jax.ShapeDtypeStruct((m, ch), BF16),
        scratch_shapes=[pltpu.VMEM((tt + CONV_HALO, ch), F32), pltpu.VMEM((tt, ch), F32)],
        compiler_params=_cparams("parallel", "arbitrary"),
        name="conformer_conv",
    )(y, y, y, y, conv_w, row(conv_b), row(ln_g), row(ln_b))


def _outproj_kernel(a_ref, b_ref, wa_ref, wb_ref, x_ref, g_ref, o_ref, *, a_residue_major):
    a = a_ref[...]
    if a_residue_major:
        a = _dot(_residue_perm_matrix(a.shape[0], True), a).astype(BF16)
    y = _dot(a, wa_ref[...]) + _dot(b_ref[...], wb_ref[...])
    o_ref[...] = x_ref[...] + g_ref[...] * y


def out_proj(a, b, w_stack, layer, x2, ada3, g_idx, seq, a_residue_major=False, tm=512):
    m, d = x2.shape
    ka = a.shape[1]
    kb = b.shape[1]
    assert ka == kb
    assert not a_residue_major or tm == PERM_TILE
    tiles_per_batch = seq // tm
    return pl.pallas_call(
        functools.partial(_outproj_kernel, a_residue_major=a_residue_major),
        grid=(m // tm,),
        in_specs=[
            pl.BlockSpec((tm, ka), lambda i: (i, 0)),
            pl.BlockSpec((tm, kb), lambda i: (i, 0)),
            pl.BlockSpec((None, ka, d), lambda i: (layer, 0, 0)),
            pl.BlockSpec((None, kb, d), lambda i: (layer, 1, 0)),
            pl.BlockSpec((tm, d), lambda i: (i, 0)),
            pl.BlockSpec((None, 1, d), lambda i: (i // tiles_per_batch, 0, g_idx)),
        ],
        out_specs=pl.BlockSpec((tm, d), lambda i: (i, 0)),
        out_shape=jax.ShapeDtypeStruct((m, d), F32),
        compiler_params=_cparams("parallel"),
        name="out_proj",
    )(a, b, w_stack, w_stack, x2, ada3)


def _ffn_kernel(x_ref, g_ref, sc_ref, sh_ref, gate_ref, wg_ref, wu_ref, wd_ref, o_ref, h_scr, acc_scr):
    f = pl.program_id(1)

    @pl.when(f == 0)
    def _():
        h_scr[...] = _norm_mod(x_ref[...], g_ref[...], sc_ref[...], sh_ref[...]).astype(BF16)
        acc_scr[...] = jnp.zeros(acc_scr.shape, F32)

    h = h_scr[...]
    a = _silu(_dot(h, wg_ref[...])) * _dot(h, wu_ref[...])
    acc_scr[...] += _dot(a.astype(BF16), wd_ref[...])

    @pl.when(f == pl.num_programs(1) - 1)
    def _():
        o_ref[...] = x_ref[...] + gate_ref[...] * acc_scr[...]


def ffn_dense(x2, norm_g, ada3, sc_idx, sh_idx, g_idx, wg, wu, wd, layer, seq, tm=512, tf=512):
    m, d = x2.shape
    dff = wg.shape[2]
    tiles_per_batch = seq // tm
    mod = lambda k: pl.BlockSpec((None, 1, d), lambda i, f: (i // tiles_per_batch, 0, k))
    return pl.pallas_call(
        _ffn_kernel,
        grid=(m // tm, dff // tf),
        in_specs=[
            pl.BlockSpec((tm, d), lambda i, f: (i, 0)),
            pl.BlockSpec((1, d), lambda i, f: (0, 0)),
            mod(sc_idx), mod(sh_idx), mod(g_idx),
            pl.BlockSpec((None, d, tf), lambda i, f: (layer, 0, f)),
            pl.BlockSpec((None, d, tf), lambda i, f: (layer, 0, f)),
            pl.BlockSpec((None, tf, d), lambda i, f: (layer, f, 0)),
        ],
        out_specs=pl.BlockSpec((tm, d), lambda i, f: (i, 0)),
        out_shape=jax.ShapeDtypeStruct((m, d), F32),
        scratch_shapes=[pltpu.VMEM((tm, d), BF16), pltpu.VMEM((tm, d), F32)],
        compiler_params=_cparams("parallel", "arbitrary"),
        name="ffn_dense",
    )(x2, norm_g.reshape(1, d), ada3, ada3, ada3, wg, wu, wd)


def _router_kernel(x_ref, g_ref, sc_ref, sh_ref, rw_ref, cols_ref, rows_ref, counts_ref, *, n_experts):
    i = pl.program_id(0)

    @pl.when(i == 0)
    def _():
        counts_ref[...] = jnp.zeros(counts_ref.shape, F32)

    h = _norm_mod(x_ref[...], g_ref[...], sc_ref[...], sh_ref[...])
    rw = rw_ref[...]
    h_hi, rw_hi = h.astype(BF16), rw.astype(BF16)
    h_lo, rw_lo = (h - h_hi.astype(F32)).astype(BF16), (rw - rw_hi.astype(F32)).astype(BF16)
    logits = _dot(h_hi, rw_hi) + _dot(h_hi, rw_lo) + _dot(h_lo, rw_hi)
    tm = logits.shape[0]
    lane = lax.broadcasted_iota(I32, logits.shape, 1)
    lg = jnp.where(lane < n_experts, logits, NEG_INF)
    m1 = jnp.max(lg, axis=-1, keepdims=True)
    i1 = jnp.min(jnp.where(lg == m1, lane, LANES), axis=-1, keepdims=True)
    sel1 = lane == i1
    lg2 = jnp.where(sel1, NEG_INF, lg)
    m2 = jnp.max(lg2, axis=-1, keepdims=True)
    i2 = jnp.min(jnp.where(lg2 == m2, lane, LANES), axis=-1, keepdims=True)
    sel2 = lane == i2
    e = jnp.exp(m2 - m1)
    w1 = 1.0 / (1.0 + e)
    w2 = e / (1.0 + e)

    onehot = jnp.where(jnp.logical_or(sel1, sel2), 1.0, 0.0)
    tri = jnp.where(lax.broadcasted_iota(I32, (tm, tm), 0) > lax.broadcasted_iota(I32, (tm, tm), 1), 1.0, 0.0)
    rank = counts_ref[0:1, :] + _dot(tri.astype(BF16), onehot.astype(BF16))
    rank1 = jnp.sum(jnp.where(sel1, rank, 0.0), axis=-1, keepdims=True)
    rank2 = jnp.sum(jnp.where(sel2, rank, 0.0), axis=-1, keepdims=True)
    counts_ref[0:1, :] = counts_ref[0:1, :] + jnp.sum(onehot, axis=0, keepdims=True)

    cols = jnp.zeros(logits.shape, F32)
    for k, val in enumerate((rank1, rank2, i1.astype(F32), i2.astype(F32), w1, w2)):
        cols = jnp.where(lane == k, val, cols)
    cols_ref[...] = cols
    rows_ref[...] = jnp.transpose(cols)[0:SUBLANES, :].astype(I32)


def moe_router(x2, norm_g, ada3, sc_idx, sh_idx, router_w, seq, tm=512):
    m, d = x2.shape
    n_experts = router_w.shape[1]
    rw = jnp.zeros((d, LANES), F32).at[:, :n_experts].set(router_w)
    tiles_per_batch = seq // tm
    mod = lambda k: pl.BlockSpec((None, 1, d), lambda i: (i // tiles_per_batch, 0, k))
    return pl.pallas_call(
        functools.partial(_router_kernel, n_experts=n_experts),
        grid=(m // tm,),
        in_specs=[
            pl.BlockSpec((tm, d), lambda i: (i, 0)),
            pl.BlockSpec((1, d), lambda i: (0, 0)),
            mod(sc_idx), mod(sh_idx),
            pl.BlockSpec((d, LANES), lambda i: (0, 0)),
        ],
        out_specs=[
            pl.BlockSpec((tm, LANES), lambda i: (i, 0)),
            pl.BlockSpec((SUBLANES, tm), lambda i: (i, 0)),
            pl.BlockSpec((SUBLANES, LANES), lambda i: (0, 0)),
        ],
        out_shape=[
            jax.ShapeDtypeStruct((m, LANES), F32),
            jax.ShapeDtypeStruct((m // tm * SUBLANES, tm), I32),
            jax.ShapeDtypeStruct((SUBLANES, LANES), F32),
        ],
        compiler_params=_cparams("arbitrary"),
        name="moe_router",
    )(x2, norm_g.reshape(1, d), ada3, ada3, rw)


def _row_copy(src, src_row, dst, dst_row, sem):
    return pltpu.make_async_copy(src.at[pl.ds(src_row, 1)], dst.at[pl.ds(dst_row, 1)], sem)


def _dispatch_kernel(x_ref, g_ref, sc_ref, sh_ref, pos_ref, xs_in_ref, xs_ref, hp_scr, pos_smem, sem_pos, sem_rows):
    del xs_in_ref
    tm, half = hp_scr.shape
    pos_copy = pltpu.make_async_copy(pos_ref, pos_smem, sem_pos)
    pos_copy.start()
    h = _norm_mod(x_ref[...], g_ref[...], sc_ref[...], sh_ref[...])
    hp_scr[...] = _pack_bf16_pair(h[:, :half], h[:, half:])
    pos_copy.wait()

    def issue(t, carry):
        _row_copy(hp_scr, t, xs_ref, pos_smem[0, t], sem_rows).start()
        _row_copy(hp_scr, t, xs_ref, pos_smem[1, t], sem_rows).start()
        return carry

    lax.fori_loop(0, tm, issue, 0, unroll=ROW_DMA_UNROLL)

    def drain(t, carry):
        _row_copy(hp_scr, t, xs_ref, pos_smem[0, t], sem_rows).wait()
        _row_copy(hp_scr, t, xs_ref, pos_smem[1, t], sem_rows).wait()
        return carry

    lax.fori_loop(0, tm, drain, 0, unroll=ROW_DMA_UNROLL)


def moe_dispatch(x2, norm_g, ada3, sc_idx, sh_idx, pos_rows, n_rows, seq, tm=512):
    m, d = x2.shape
    half = d // 2
    tiles_per_batch = seq // tm
    mod = lambda k: pl.BlockSpec((None, 1, d), lambda i: (i // tiles_per_batch, 0, k))
    xs0 = jnp.zeros((n_rows, half), U32)
    return pl.pallas_call(
        _dispatch_kernel,
        grid=(m // tm,),
        in_specs=[
            pl.BlockSpec((tm, d), lambda i: (i, 0)),
            pl.BlockSpec((1, d), lambda i: (0, 0)),
            mod(sc_idx), mod(sh_idx),
            pl.BlockSpec((SUBLANES, tm), lambda i: (i, 0)),
            pl.BlockSpec(memory_space=pl.ANY),
        ],
        out_specs=pl.BlockSpec(memory_space=pl.ANY),
        out_shape=jax.ShapeDtypeStruct((n_rows, half), U32),
        scratch_shapes=[
            pltpu.VMEM((tm, half), U32), pltpu.SMEM((SUBLANES, tm), I32),
            pltpu.SemaphoreType.DMA, pltpu.SemaphoreType.DMA,
        ],
        input_output_aliases={5: 0},
        compiler_params=_cparams("arbitrary"),
        name="moe_dispatch",
    )(x2, norm_g.reshape(1, d), ada3, ada3, pos_rows, xs0)


def _moe_ffn_kernel(te_ref, tv_ref, xs_ref, wg_ref, wu_ref, wd_ref, ys_ref, h_scr, acc_scr):
    del te_ref
    g = pl.program_id(0)
    f = pl.program_id(1)
    half = xs_ref.shape[1]
    used = tv_ref[g] > 0
    last = f == pl.num_programs(1) - 1

    @pl.when(jnp.logical_and(used, f == 0))
    def _():
        lo, hi = _unpack_bf16_pair(xs_ref[...])
        h_scr[:, :half] = lo.astype(BF16)
        h_scr[:, half:] = hi.astype(BF16)
        acc_scr[...] = jnp.zeros(acc_scr.shape, F32)

    @pl.when(used)
    def _():
        h = h_scr[...]
        a = _silu(_dot(h, wg_ref[...])) * _dot(h, wu_ref[...])
        acc_scr[...] += _dot(a.astype(BF16), wd_ref[...])

    @pl.when(jnp.logical_and(used, last))
    def _():
        ys_ref[...] = _pack_bf16_pair(acc_scr[:, :half], acc_scr[:, half:])

    @pl.when(jnp.logical_and(jnp.logical_not(used), last))
    def _():
        ys_ref[...] = jnp.zeros(ys_ref.shape, U32)


def moe_expert_ffn(xs, tile_expert, tile_used, wg, wu, wd, layer, tm, tf=512):
    n_rows, half = xs.shape
    d = 2 * half
    dff = wg.shape[3]
    grid_spec = pltpu.PrefetchScalarGridSpec(
        num_scalar_prefetch=2,
        grid=(n_rows // tm, dff // tf),
        in_specs=[
            pl.BlockSpec((tm, half), lambda g, f, te, tv: (g, 0)),
            pl.BlockSpec((None, None, d, tf), lambda g, f, te, tv: (layer, te[g], 0, f * tv[g])),
            pl.BlockSpec((None, None, d, tf), lambda g, f, te, tv: (layer, te[g], 0, f * tv[g])),
            pl.BlockSpec((None, None, tf, d), lambda g, f, te, tv: (layer, te[g], f * tv[g], 0)),
        ],
        out_specs=pl.BlockSpec((tm, half), lambda g, f, te, tv: (g, 0)),
        scratch_shapes=[pltpu.VMEM((tm, d), BF16), pltpu.VMEM((tm, d), F32)],
    )
    return pl.pallas_call(
        _moe_ffn_kernel,
        grid_spec=grid_spec,
        out_shape=jax.ShapeDtypeStruct((n_rows, half), U32),
        compiler_params=_cparams("parallel", "arbitrary"),
        name="moe_expert_ffn",
    )(tile_expert, tile_used, xs, wg, wu, wd)


def _combine_kernel(x_ref, gate_ref, cols_ref, pos_ref, ys_ref, o_ref, r1_scr, r2_scr, pos_smem, sem_pos, sem_rows):
    tm, half = r1_scr.shape
    pos_copy = pltpu.make_async_copy(pos_ref, pos_smem, sem_pos)
    pos_copy.start()
    pos_copy.wait()

    def issue(t, carry):
        _row_copy(ys_ref, pos_smem[0, t], r1_scr, t, sem_rows).start()
        _row_copy(ys_ref, pos_smem[1, t], r2_scr, t, sem_rows).start()
        return carry

    lax.fori_loop(0, tm, issue, 0, unroll=ROW_DMA_UNROLL)

    def drain(t, carry):
        _row_copy(ys_ref, pos_smem[0, t], r1_scr, t, sem_rows).wait()
        _row_copy(ys_ref, pos_smem[1, t], r2_scr, t, sem_rows).wait()
        return carry

    lax.fori_loop(0, tm, drain, 0, unroll=ROW_DMA_UNROLL)

    w1 = cols_ref[:, 4:5]
    w2 = cols_ref[:, 5:6]
    lo1, hi1 = _unpack_bf16_pair(r1_scr[...])
    lo2, hi2 = _unpack_bf16_pair(r2_scr[...])
    o_ref[:, :half] = x_ref[:, :half] + gate_ref[:, :half] * (w1 * lo1 + w2 * lo2)
    o_ref[:, half:] = x_ref[:, half:] + gate_ref[:, half:] * (w1 * hi1 + w2 * hi2)


def moe_combine(x2, ada3, g_idx, cols, pos_rows, ys, seq, tm=512):
    m, d = x2.shape
    half = d // 2
    tiles_per_batch = seq // tm
    return pl.pallas_call(
        _combine_kernel,
        grid=(m // tm,),
        in_specs=[
            pl.BlockSpec((tm, d), lambda i: (i, 0)),
            pl.BlockSpec((None, 1, d), lambda i: (i // tiles_per_batch, 0, g_idx)),
            pl.BlockSpec((tm, LANES), lambda i: (i, 0)),
            pl.BlockSpec((SUBLANES, tm), lambda i: (i, 0)),
            pl.BlockSpec(memory_space=pl.ANY),
        ],
        out_specs=pl.BlockSpec((tm, d), lambda i: (i, 0)),
        out_shape=jax.ShapeDtypeStruct((m, d), F32),
        scratch_shapes=[
            pltpu.VMEM((tm, half), U32), pltpu.VMEM((tm, half), U32), pltpu.SMEM((SUBLANES, tm), I32),
            pltpu.SemaphoreType.DMA, pltpu.SemaphoreType.DMA,
        ],
        compiler_params=_cparams("arbitrary"),
        name="moe_combine",
    )(x2, ada3, cols, pos_rows, ys)


def moe_ffn(x2, norm_g, ada3, sc_idx, sh_idx, g_idx, router_w, wg, wu, wd, layer, seq, tm=512,
            row_tile=MOE_ROW_TILE):
    m = x2.shape[0]
    n_experts = router_w.shape[1]
    row_tile = min(row_tile, m)
    cols, rows, counts = moe_router(x2, norm_g, ada3, sc_idx, sh_idx, router_w, seq, tm=tm)
    n_rows = 2 * m + n_experts * row_tile
    counts = counts[0, :n_experts].astype(I32)
    padded = ((counts + row_tile - 1) // row_tile) * row_tile
    ends = jnp.cumsum(padded)
    offs = ends - padded
    rows3 = rows.reshape(m // tm, SUBLANES, tm)
    pos_rows = rows3.at[:, 0].add(offs[rows3[:, 2]]).at[:, 1].add(offs[rows3[:, 3]]).reshape(rows.shape)
    tile_start = jnp.arange(n_rows // row_tile, dtype=I32) * row_tile
    tile_expert = jnp.minimum(jnp.sum(tile_start[:, None] >= ends[None, :], axis=1), n_experts - 1).astype(I32)
    tile_used = (tile_start < ends[-1]).astype(I32)

    xs = moe_dispatch(x2, norm_g, ada3, sc_idx, sh_idx, pos_rows, n_rows, seq, tm=tm)
    ys = moe_expert_ffn(xs, tile_expert, tile_used, wg, wu, wd, layer, row_tile)
    return moe_combine(x2, ada3, g_idx, cols, pos_rows, ys, seq, tm=tm)


def _head_gain(q_g, k_g, n_q_heads, n_k_heads, q_scale):
    return jnp.concatenate([jnp.tile(q_g * q_scale, n_q_heads), jnp.tile(k_g, n_k_heads)]).reshape(1, -1).astype(F32)


def kernel(x, c, ada_w, ada_b, norm_mix_g, norm_ffn_g, ev_w_in, ev_w_out, a_q_norm_g, a_k_norm_g, a_lambda,
           a_head_norm_g, ffn_w_gate, ffn_w_up, ffn_w_down, od_w_in, od_w_out, c_q_norm_g, c_k_norm_g,
           d_conv_w, d_conv_b, d_ln_g, d_ln_b, moe_router_w, moe_w_gate, moe_w_up, moe_w_down):
    batch, seq, d = x.shape
    depth = ada_w.shape[0]
    m = batch * seq
    scale = HEAD_DIM ** -0.5
    a_width = ev_w_out.shape[1] // 2
    a_heads = a_width // (2 * HEAD_DIM)
    b_heads = a_width // HEAD_DIM
    c_width = od_w_out.shape[1] // 2
    c_heads = c_width // HEAD_DIM

    ev_w_in, ev_w_out, od_w_in, od_w_out = (w.astype(BF16) for w in (ev_w_in, ev_w_out, od_w_in, od_w_out))
    ffn_w = tuple(w.astype(BF16) for w in (ffn_w_gate, ffn_w_up, ffn_w_down))
    moe_w = tuple(w.astype(BF16) for w in (moe_w_gate, moe_w_up, moe_w_down))

    ada = ada_all(c, ada_w, ada_b)
    x2 = x.reshape(m, d)
    for i in range(depth):
        j = i // 2
        ada3 = ada[i].reshape(batch, 1, 6 * d)
        if i % 2 == 0:
            lam_init = 0.8 - 0.6 * math.exp(-0.3 * i)
            lv = a_lambda[j].astype(F32)
            lam = jnp.exp(jnp.sum(lv[0] * lv[1])) - jnp.exp(jnp.sum(lv[2] * lv[3])) + lam_init
            slopes = 2.0 ** (-8.0 * jnp.arange(1, a_heads + 1, dtype=F32) / a_heads)
            gain = _head_gain(a_q_norm_g[j], a_k_norm_g[j], 2 * a_heads, 2 * a_heads, scale * LOG2E)
            (y,) = in_proj(x2, norm_mix_g[i], ada3, 1, 0, ev_w_in, j, gain, seq)
            oa = diff_attention(y, slopes, lam.reshape(1), a_head_norm_g[j], batch, seq, a_heads,
                                k_col0=a_width, v_col0=2 * a_width, out_scale=1.0 - lam_init)
            ob = stick_attention(y, batch, seq, b_heads, q_col0=3 * a_width, k_col0=4 * a_width,
                                 v_col0=5 * a_width)
            x2 = out_proj(oa, ob, ev_w_out, j, x2, ada3, 2, seq)
            x2 = ffn_dense(x2, norm_ffn_g[i], ada3, 4, 3, 5, *ffn_w, j, seq)
        else:
            gain = _head_gain(c_q_norm_g[j], c_k_norm_g[j], c_heads, c_heads, scale)
            qkv, du = in_proj(x2, norm_mix_g[i], ada3, 1, 0, od_w_in, j, gain, seq, n_first=3 * c_width,
                              perm_first=True)
            oc = dilated_mixture(qkv, batch, seq, c_heads)
            od = conformer_conv(du, d_conv_w[j], d_conv_b[j], d_ln_g[j], d_ln_b[j], batch, seq, a_col0=0)
            x2 = out_proj(oc, od, od_w_out, j, x2, ada3, 2, seq, a_residue_major=True)
            x2 = moe_ffn(x2, norm_ffn_g[i], ada3, 4, 3, 5, moe_router_w[j], *moe_w, j, seq)
    return x2.reshape(batch, seq, d)
```

```python
import functools
import math

import jax
import jax.numpy as jnp
from jax import lax
from jax.experimental import pallas as pl
from jax.experimental.pallas import tpu as pltpu

F32 = jnp.float32
BF16 = jnp.bfloat16
U32 = jnp.uint32
I32 = jnp.int32
EPS = 1e-6
HEAD_DIM = 128
LANES = 128
SUBLANES = 8
VMEM_LIMIT_BYTES = 56 * 1024 * 1024
CONV_HALO = 32
STICK_UNDERFLOW = -104.0
DILATED = ((128, 1), (512, 4), (2048, 16))
NEG_INF = float("-inf")
LOG2E = math.log2(math.e)
MOE_ROW_TILE = 1024
ROW_DMA_UNROLL = 8
RESIDUES = 16
PERM_TILE = 512
BAND_STEPS = 128


def _cparams(*sem):
    return pltpu.CompilerParams(dimension_semantics=sem, vmem_limit_bytes=VMEM_LIMIT_BYTES)


def _dot(a, b):
    return jnp.dot(a, b, preferred_element_type=F32)


def _dot_nt(a, b):
    return lax.dot_general(a, b, (((1,), (1,)), ((), ())), preferred_element_type=F32)


def _silu(x):
    return x * (1.0 / (1.0 + jnp.exp(-x)))


def _norm_mod(x, g, sc, sh):
    ms = jnp.mean(x * x, axis=-1, keepdims=True)
    return (x * lax.rsqrt(ms + EPS) * g) * (1.0 + sc) + sh


def _pack_bf16_pair(lo, hi):
    lo_bits = lax.bitcast_convert_type(lo.astype(BF16).astype(F32), U32)
    hi_bits = lax.bitcast_convert_type(hi.astype(BF16).astype(F32), U32)
    return hi_bits | (lo_bits >> 16)


def _unpack_bf16_pair(w):
    lo = lax.bitcast_convert_type(w << 16, F32)
    hi = lax.bitcast_convert_type(w & U32(0xFFFF0000), F32)
    return lo, hi


def _ada_kernel(c_ref, w_ref, b_ref, o_ref):
    cond = _silu(c_ref[...])
    o_ref[...] = jnp.dot(cond, w_ref[...], preferred_element_type=F32,
                         precision=lax.Precision.HIGHEST) + b_ref[...]


def ada_all(c, ada_w, ada_b, tn=1024):
    depth, d, n = ada_w.shape
    b = c.shape[0]
    return pl.pallas_call(
        _ada_kernel,
        grid=(depth, n // tn),
        in_specs=[
            pl.BlockSpec((b, d), lambda i, j: (0, 0)),
            pl.BlockSpec((None, d, tn), lambda i, j: (i, 0, j)),
            pl.BlockSpec((None, 1, tn), lambda i, j: (i, 0, j)),
        ],
        out_specs=pl.BlockSpec((None, b, tn), lambda i, j: (i, 0, j)),
        out_shape=jax.ShapeDtypeStruct((depth, b, n), F32),
        compiler_params=_cparams("parallel", "parallel"),
        name="ada",
    )(c, ada_w, ada_b.reshape(depth, 1, n))


def _residue_perm_matrix(n, inverse):
    c = n // RESIDUES
    i = lax.broadcasted_iota(I32, (n, n), 1 if inverse else 0)
    t = lax.broadcasted_iota(I32, (n, n), 0 if inverse else 1)
    return jnp.where(t == _interleave_pos(i, c, RESIDUES), 1.0, 0.0).astype(BF16)


def _interleave_pos(i, inner, outer):
    shift = inner.bit_length() - 1
    assert inner == 1 << shift
    return (i & (inner - 1)) * outer + (i >> shift)


def _inproj_kernel(x_ref, g_ref, sc_ref, sh_ref, w_ref, hg_ref, *rest, n_out, n_norm_tiles, n_first_tiles,
                   perm_first):
    o_refs, scr = rest[:n_out], rest[n_out:]
    h_scr = scr[0]
    first_lhs = scr[1] if perm_first else h_scr
    j = pl.program_id(1)

    @pl.when(j == 0)
    def _():
        h = _norm_mod(x_ref[...], g_ref[...], sc_ref[...], sh_ref[...]).astype(BF16)
        h_scr[...] = h
        if perm_first:
            first_lhs[...] = _dot(_residue_perm_matrix(h.shape[0], False), h).astype(BF16)

    @pl.when(j < n_first_tiles)
    def _():
        y = _dot(first_lhs[...], w_ref[...])
        tn = y.shape[1]

        @pl.when(j < n_norm_tiles)
        def _():
            for c in range(tn // HEAD_DIM):
                sl = slice(c * HEAD_DIM, (c + 1) * HEAD_DIM)
                blk = y[:, sl]
                r = lax.rsqrt(jnp.mean(blk * blk, axis=-1, keepdims=True) + EPS)
                o_refs[0][:, sl] = (blk * r * hg_ref[:, sl]).astype(BF16)

        @pl.when(j >= n_norm_tiles)
        def _():
            o_refs[0][...] = y.astype(BF16)

    if n_out > 1:
        @pl.when(j >= n_first_tiles)
        def _():
            o_refs[1][...] = _dot(h_scr[...], w_ref[...]).astype(BF16)


def in_proj(x2, norm_g, ada3, sc_idx, sh_idx, w_stack, layer, head_gain, seq, n_first=None, perm_first=False,
            tm=512, tn=1024):
    m, d = x2.shape
    n = w_stack.shape[2]
    n_first = n if n_first is None else n_first
    n_norm_tiles = head_gain.shape[1] // tn
    n_first_tiles = n_first // tn
    assert n_norm_tiles <= n_first_tiles
    assert not perm_first or tm == PERM_TILE
    tiles_per_batch = seq // tm
    out_specs = [pl.BlockSpec((tm, tn), lambda i, j: (i, jnp.minimum(j, n_first_tiles - 1)))]
    out_shape = [jax.ShapeDtypeStruct((m, n_first), BF16)]
    if n_first < n:
        out_specs.append(pl.BlockSpec((tm, tn), lambda i, j: (i, jnp.maximum(j - n_first_tiles, 0))))
        out_shape.append(jax.ShapeDtypeStruct((m, n - n_first), BF16))
    return pl.pallas_call(
        functools.partial(_inproj_kernel, n_out=len(out_specs), n_norm_tiles=n_norm_tiles,
                          n_first_tiles=n_first_tiles, perm_first=perm_first),
        grid=(m // tm, n // tn),
        in_specs=[
            pl.BlockSpec((tm, d), lambda i, j: (i, 0)),
            pl.BlockSpec((1, d), lambda i, j: (0, 0)),
            pl.BlockSpec((None, 1, d), lambda i, j: (i // tiles_per_batch, 0, sc_idx)),
            pl.BlockSpec((None, 1, d), lambda i, j: (i // tiles_per_batch, 0, sh_idx)),
            pl.BlockSpec((None, d, tn), lambda i, j: (layer, 0, j)),
            pl.BlockSpec((1, tn), lambda i, j: (0, jnp.minimum(j, n_norm_tiles - 1))),
        ],
        out_specs=out_specs,
        out_shape=out_shape,
        scratch_shapes=[pltpu.VMEM((tm, d), BF16)] * (2 if perm_first else 1),
        compiler_params=_cparams("parallel", "arbitrary"),
        name="in_proj",
    )(x2, norm_g.reshape(1, d), ada3, ada3, w_stack, head_gain)


def _diff_attn_kernel(slopes_ref, lam_ref, q_ref, k_ref, v_ref, hg_ref, o_ref, m_scr, l_scr, a_scr,
                      *, tq, out_scale, heads):
    hg0 = pl.program_id(1) * heads
    qi = pl.program_id(2)
    hw = 2 * HEAD_DIM
    lam = lam_ref[0]
    col = lax.broadcasted_iota(I32, (1, tq), 1).astype(F32)
    reps = tq // LANES
    maps = [(h, a) for h in range(heads) for a in range(2)]
    qsl = [slice(h * hw + a * HEAD_DIM, h * hw + (a + 1) * HEAD_DIM) for h, a in maps]
    vsl = [slice(h * hw, (h + 1) * hw) for h, _ in maps]

    m_scr[...] = jnp.full(m_scr.shape, NEG_INF, F32)
    l_scr[...] = jnp.zeros(l_scr.shape, F32)
    a_scr[...] = jnp.zeros(a_scr.shape, F32)

    def block(kb, masked):
        ks = pl.multiple_of(kb * tq, tq)
        kpos = col + ((kb - qi) * tq).astype(F32)
        cbias = [slopes_ref[hg0 + h] * LOG2E * kpos for h in range(heads)]
        ss = [_dot_nt(q_ref[:, sl], k_ref[pl.ds(ks, tq), sl]) + cbias[h] for sl, (h, _) in zip(qsl, maps)]
        if masked:
            rc = lax.broadcasted_iota(I32, (tq, tq), 0) - lax.broadcasted_iota(I32, (tq, tq), 1)
            ss = [jnp.where(rc >= 0, s, NEG_INF) for s in ss]
        m_olds = [m_scr[i] for i in range(len(maps))]
        m_news = [jnp.maximum(m_old, jnp.max(s, axis=-1, keepdims=True)) for m_old, s in zip(m_olds, ss)]
        ps = [jnp.exp2(s - jnp.tile(m_new, (1, reps))) for s, m_new in zip(ss, m_news)]
        pvs = [_dot(p.astype(BF16), v_ref[pl.ds(ks, tq), sl]) for p, sl in zip(ps, vsl)]
        for i in range(len(maps)):
            alpha = jnp.exp2(m_olds[i] - m_news[i])
            l_scr[i] = alpha * l_scr[i] + jnp.sum(ps[i], axis=-1, keepdims=True)
            a_scr[i] = jnp.tile(alpha, (1, 2)) * a_scr[i] + pvs[i]
            m_scr[i] = m_news[i]

    def body(kb, carry):
        block(kb, False)
        return carry

    lax.fori_loop(0, qi, body, 0)
    block(qi, True)

    for h in range(heads):
        inv1 = jnp.tile(1.0 / l_scr[2 * h], (1, 2))
        inv2 = jnp.tile(1.0 / l_scr[2 * h + 1], (1, 2))
        o = a_scr[2 * h] * inv1 - lam * (a_scr[2 * h + 1] * inv2)
        r = lax.rsqrt(jnp.mean(o * o, axis=-1, keepdims=True) + EPS)
        o_ref[:, h * hw:(h + 1) * hw] = (o * r * hg_ref[...] * out_scale).astype(BF16)


def diff_attention(y, slopes, lam, head_g, batch, seq, n_heads, k_col0, v_col0, out_scale, tq=512, heads=2):
    m = y.shape[0]
    hw = 2 * HEAD_DIM
    bw = heads * hw
    nq = seq // tq
    kernel = functools.partial(_diff_attn_kernel, tq=tq, out_scale=out_scale, heads=heads)
    return pl.pallas_call(
        kernel,
        grid=(batch, n_heads // heads, nq),
        in_specs=[
            pl.BlockSpec(memory_space=pltpu.SMEM),
            pl.BlockSpec(memory_space=pltpu.SMEM),
            pl.BlockSpec((tq, bw), lambda b, h, qi: (b * nq + qi, h)),
            pl.BlockSpec((seq, bw), lambda b, h, qi: (b, k_col0 // bw + h)),
            pl.BlockSpec((seq, bw), lambda b, h, qi: (b, v_col0 // bw + h)),
            pl.BlockSpec((1, hw), lambda b, h, qi: (0, 0)),
        ],
        out_specs=pl.BlockSpec((tq, bw), lambda b, h, qi: (b * nq + qi, h)),
        out_shape=jax.ShapeDtypeStruct((m, n_heads * hw), BF16),
        scratch_shapes=[
            pltpu.VMEM((2 * heads, tq, LANES), F32), pltpu.VMEM((2 * heads, tq, LANES), F32),
            pltpu.VMEM((2 * heads, tq, hw), F32),
        ],
        compiler_params=_cparams("parallel", "parallel", "arbitrary"),
        name="diff_attn",
    )(slopes, lam, y, y, y, head_g.reshape(1, hw))


def _stick_kernel(q_ref, k_ref, v_ref, o_ref, *scratch, tq, tk, scale, heads):
    acc_scrs, carry_scrs = scratch[:heads], scratch[heads:]
    qi = pl.program_id(2)
    rc = lax.broadcasted_iota(I32, (tq, tk), 0) - lax.broadcasted_iota(I32, (tq, tk), 1)
    jj = lax.broadcasted_iota(I32, (tk, 2 * tk), 0)
    ss = lax.broadcasted_iota(I32, (tk, 2 * tk), 1)
    suffix = jnp.where(jnp.logical_or(jj > ss, ss >= tk), 1.0, 0.0).astype(BF16)

    for ref in scratch:
        ref[...] = jnp.zeros(ref.shape, F32)
    kb0 = (qi * tq + tq) // tk - 1

    def cond(state):
        kb, live = state
        return jnp.logical_and(kb >= 0, live > STICK_UNDERFLOW)

    def body(state):
        kb, _ = state
        ks = pl.multiple_of(kb * tk, tk)
        past = rc + (qi * tq - kb * tk) > 0
        sls = [slice(h * HEAD_DIM, (h + 1) * HEAD_DIM) for h in range(heads)]
        zs = [_dot_nt((q_ref[:, sl].astype(F32) * scale).astype(BF16), k_ref[pl.ds(ks, tk), sl]) for sl in sls]
        lsigs = [jnp.minimum(z, 0.0) - jnp.log(1.0 + jnp.exp(-jnp.abs(z))) for z in zs]
        l1ms = [jnp.where(past, lsig - z, 0.0) for lsig, z in zip(lsigs, zs)]
        his = [l1m.astype(BF16) for l1m in l1ms]
        los = [(l1m - hi.astype(F32)).astype(BF16) for l1m, hi in zip(l1ms, his)]
        sums = [_dot(hi, suffix) + _dot(lo, suffix) for hi, lo in zip(his, los)]
        carries = [ref[...] for ref in carry_scrs]
        ws = [jnp.where(past, jnp.exp(lsig + carry + sm[:, :tk]), 0.0).astype(BF16)
              for lsig, carry, sm in zip(lsigs, carries, sums)]
        pvs = [_dot(w, v_ref[pl.ds(ks, tk), sl]) for w, sl in zip(ws, sls)]
        live = jnp.full((tq, tk), NEG_INF, F32)
        for h in range(heads):
            acc_scrs[h][...] += pvs[h]
            carry = carries[h] + sums[h][:, tk:]
            carry_scrs[h][...] = carry
            live = jnp.maximum(live, carry)
        return kb - 1, jnp.max(live)

    lax.while_loop(cond, body, (kb0, jnp.float32(0.0)))
    for h in range(heads):
        o_ref[:, h * HEAD_DIM:(h + 1) * HEAD_DIM] = acc_scrs[h][...].astype(BF16)


def stick_attention(y, batch, seq, n_heads, q_col0, k_col0, v_col0, tq=256, tk=128, heads=4):
    m = y.shape[0]
    nq = seq // tq
    bw = heads * HEAD_DIM
    assert tk == LANES
    kernel = functools.partial(_stick_kernel, tq=tq, tk=tk, scale=HEAD_DIM ** -0.5, heads=heads)
    return pl.pallas_call(
        kernel,
        grid=(batch, n_heads // heads, nq),
        in_specs=[
            pl.BlockSpec((tq, bw), lambda b, h, qi: (b * nq + qi, q_col0 // bw + h)),
            pl.BlockSpec((seq, bw), lambda b, h, qi: (b, k_col0 // bw + h)),
            pl.BlockSpec((seq, bw), lambda b, h, qi: (b, v_col0 // bw + h)),
        ],
        out_specs=pl.BlockSpec((tq, bw), lambda b, h, qi: (b * nq + qi, h)),
        out_shape=jax.ShapeDtypeStruct((m, n_heads * HEAD_DIM), BF16),
        scratch_shapes=[pltpu.VMEM((tq, HEAD_DIM), F32)] * heads + [pltpu.VMEM((tq, tk), F32)] * heads,
        compiler_params=_cparams("parallel", "parallel", "arbitrary"),
        name="stick_attn",
    )(y, y, y)


def _dilated_kernel(*refs, n_heads, dilation, has_state, final, qb, pos_inner, pos_outer):
    if has_state:
        q_ref, kc_ref, kp_ref, vc_ref, vp_ref, acc_in_ref, m_in_ref, l_in_ref = refs[:8]
        outs = refs[8:]
    else:
        q_ref, kc_ref, kp_ref, vc_ref, vp_ref = refs[:5]
        outs = refs[5:]
    nb = pl.program_id(2)
    blk_shape = q_ref.shape

    def rows2d(ref):
        return ref[...].reshape(qb, ref.shape[-1])

    q, kc, kp, vc, vp = (rows2d(r) for r in (q_ref, kc_ref, kp_ref, vc_ref, vp_ref))
    row = lax.broadcasted_iota(I32, (qb, qb), 0)
    col = lax.broadcasted_iota(I32, (qb, qb), 1)
    if pos_outer > 1:
        row, col = _interleave_pos(row, pos_inner, pos_outer), _interleave_pos(col, pos_inner, pos_outer)
    steps_c = row - col
    steps_p = steps_c + qb
    in_band_c = jnp.abs(steps_c - BAND_STEPS // 2) <= BAND_STEPS // 2
    in_band_p = steps_p <= jnp.where(nb > 0, BAND_STEPS, 0)
    dist_c = steps_c.astype(F32)
    dist_p = steps_p.astype(F32)
    reps = qb // HEAD_DIM

    sls = [slice(h * HEAD_DIM, (h + 1) * HEAD_DIM) for h in range(n_heads)]
    slopes = [2.0 ** (-8.0 * (h + 1) / n_heads) * dilation for h in range(n_heads)]
    s_cs = [_dot_nt(q[:, sl], kc[:, sl]) + jnp.where(in_band_c, -slope * dist_c, NEG_INF)
            for sl, slope in zip(sls, slopes)]
    s_ps = [_dot_nt(q[:, sl], kp[:, sl]) + jnp.where(in_band_p, -slope * dist_p, NEG_INF)
            for sl, slope in zip(sls, slopes)]
    m_blks = [jnp.maximum(jnp.max(s_c, axis=-1, keepdims=True), jnp.max(s_p, axis=-1, keepdims=True))
              for s_c, s_p in zip(s_cs, s_ps)]
    if has_state:
        acc_in, m_in, l_in = rows2d(acc_in_ref), rows2d(m_in_ref), rows2d(l_in_ref)
        m_news = [jnp.maximum(m_in[:, sl], m) for m, sl in zip(m_blks, sls)]
    else:
        m_news = [jnp.broadcast_to(m, (qb, HEAD_DIM)) for m in m_blks]
    m_wide = [jnp.tile(m, (1, reps)) if reps > 1 else m for m in m_news]
    p_cs = [jnp.exp(s_c - m) for s_c, m in zip(s_cs, m_wide)]
    p_ps = [jnp.exp(s_p - m) for s_p, m in zip(s_ps, m_wide)]
    accs = [_dot(p_c.astype(BF16), vc[:, sl]) + _dot(p_p.astype(BF16), vp[:, sl])
            for p_c, p_p, sl in zip(p_cs, p_ps, sls)]
    acc_cols, m_cols, l_cols = [], [], []
    for h in range(n_heads):
        l = jnp.sum(p_cs[h], axis=-1, keepdims=True) + jnp.sum(p_ps[h], axis=-1, keepdims=True)
        acc = accs[h]
        if has_state:
            r_in = jnp.exp(m_in[:, sls[h]] - m_news[h])
            l = l + l_in[:, sls[h]] * r_in
            acc = acc + acc_in[:, sls[h]] * r_in
        else:
            l = jnp.broadcast_to(l, (qb, HEAD_DIM))
        if final:
            acc_cols.append((acc / l).astype(BF16))
        else:
            acc_cols.append(acc)
            m_cols.append(m_news[h])
            l_cols.append(l)
    outs[0][...] = jnp.concatenate(acc_cols, axis=1).reshape(blk_shape)
    if not final:
        outs[1][...] = jnp.concatenate(m_cols, axis=1).reshape(blk_shape)
        outs[2][...] = jnp.concatenate(l_cols, axis=1).reshape(blk_shape)


def dilated_branch(qkv, state, batch, seq, n_heads, dilation, final):
    m, n = qkv.shape
    hw = n_heads * HEAD_DIM
    tiles = m // PERM_TILE
    tpb = seq // PERM_TILE
    rows = PERM_TILE // RESIDUES
    if dilation == RESIDUES:
        span = BAND_STEPS // rows
        assert tpb % span == 0
        qb, pos = BAND_STEPS, (1, 1)
        grid = (batch, RESIDUES, tpb // span)
        view = lambda a: a.reshape(tiles, RESIDUES, rows, a.shape[-1])
        blk = lambda w: (span, None, rows, w)
        idx = lambda b, r, i, g: (b * (tpb // span) + i, r, 0, g)
    elif dilation * BAND_STEPS == PERM_TILE:
        sub = RESIDUES // dilation
        qb, pos = BAND_STEPS, (rows, sub)
        grid = (batch, dilation, tpb)
        view = lambda a: a.reshape(tiles, sub, dilation, rows, a.shape[-1])
        blk = lambda w: (None, sub, None, rows, w)
        idx = lambda b, r, i, g: (b * tpb + i, 0, r, 0, g)
    else:
        assert dilation == 1
        qb = 2 * BAND_STEPS
        sub = qb // RESIDUES
        per_tile = rows // sub
        qb, pos = qb, (sub, RESIDUES)
        grid = (batch, 1, tpb * per_tile)
        view = lambda a: a.reshape(tiles, RESIDUES, rows, a.shape[-1])
        blk = lambda w: (None, RESIDUES, sub, w)
        idx = lambda b, r, i, g: (b * tpb + i // per_tile, 0, i % per_tile, g)

    def spec(w, g, back):
        return pl.BlockSpec(blk(w), lambda b, r, i: idx(b, r, jnp.maximum(i - back, 0), g))

    qv = view(qkv)
    in_specs = [spec(hw, 0, 0), spec(hw, 1, 0), spec(hw, 1, 1), spec(hw, 2, 0), spec(hw, 2, 1)]
    args = [qv] * 5
    if state is not None:
        in_specs += [spec(hw, 0, 0)] * 3
        args += [view(a) for a in state]

    def shape(dt):
        return jax.ShapeDtypeStruct(jax.eval_shape(view, jax.ShapeDtypeStruct((m, hw), dt)).shape, dt)

    if final:
        out_specs = [spec(hw, 0, 0)]
        out_shape = [shape(BF16)]
    else:
        out_specs = [spec(hw, 0, 0)] * 3
        out_shape = [shape(F32)] * 3
    kernel = functools.partial(_dilated_kernel, n_heads=n_heads, dilation=dilation, has_state=state is not None,
                               final=final, qb=qb, pos_inner=pos[0], pos_outer=pos[1])
    outs = pl.pallas_call(
        kernel, grid=grid, in_specs=in_specs, out_specs=out_specs, out_shape=out_shape,
        compiler_params=_cparams("parallel", "parallel", "arbitrary"),
        name=f"dilated_d{dilation}",
    )(*args)
    if final:
        return outs[0].reshape(m, hw)
    return tuple(o.reshape(m, hw) for o in outs)


def dilated_mixture(qkv, batch, seq, n_heads):
    state = None
    branches = sorted(DILATED, key=lambda wd: -wd[1])
    for idx, (window, dilation) in enumerate(branches):
        assert window // dilation == BAND_STEPS
        state = dilated_branch(qkv, state, batch, seq, n_heads, dilation, final=idx == len(branches) - 1)
    return state


def _conv_kernel(a_ref, g_ref, ap_ref, gp_ref, w_ref, cb_ref, lg_ref, lb_ref, o_ref, hs_scr, y_scr,
                 *, tt, taps, rows, norm_rows):
    i = pl.program_id(1)
    halo = hs_scr.shape[1] - tt
    n_ch = hs_scr.shape[2]
    off = halo - (taps - 1)

    def glu(a, g):
        return a.astype(F32) * (1.0 / (1.0 + jnp.exp(-g.astype(F32))))

    hs_scr[0, 0:halo, :] = jnp.where(i > 0, glu(ap_ref[...], gp_ref[...]), 0.0)
    hs_scr[0, halo:, :] = glu(a_ref[...], g_ref[...])

    taps_of = {b: [a for a in range(halo // SUBLANES + 1) if 0 <= SUBLANES * a + b - off < taps]
               for b in range(SUBLANES)}
    for b in range(1, SUBLANES):
        if taps_of[b]:
            span = tt + SUBLANES * max(taps_of[b])
            hs_scr[b, 0:span, :] = hs_scr[0, b:b + span, :]

    def conv_chunk(c, carry):
        r0 = pl.multiple_of(c * rows, rows)
        strips = [slice(s * LANES, (s + 1) * LANES) for s in range(n_ch // LANES)]
        accs = [None] * len(strips)
        for b in range(SUBLANES):
            for a in taps_of[b]:
                j = SUBLANES * a + b - off
                for s, ls in enumerate(strips):
                    w_tap = jnp.tile(w_ref[SUBLANES * j:SUBLANES * (j + 1), ls], (rows // SUBLANES, 1))
                    term = hs_scr[b, pl.ds(r0 + SUBLANES * a, rows), ls] * w_tap
                    accs[s] = term if accs[s] is None else accs[s] + term
        for s, ls in enumerate(strips):
            y_scr[pl.ds(r0, rows), ls] = accs[s]
        return carry

    lax.fori_loop(0, tt // rows, conv_chunk, 0)

    def norm_chunk(c, carry):
        r0 = pl.multiple_of(c * norm_rows, norm_rows)
        y = y_scr[pl.ds(r0, norm_rows), :] + cb_ref[...]
        mu = jnp.mean(y, axis=-1, keepdims=True)
        yc = y - mu
        var = jnp.mean(yc * yc, axis=-1, keepdims=True)
        z = yc * lax.rsqrt(var + EPS) * lg_ref[...] + lb_ref[...]
        o_ref[pl.ds(r0, norm_rows), :] = _silu(z).astype(BF16)
        return carry

    lax.fori_loop(0, tt // norm_rows, norm_chunk, 0)


def conformer_conv(y, conv_w, conv_b, ln_g, ln_b, batch, seq, a_col0, tt=256, rows=16, norm_rows=64):
    m = y.shape[0]
    taps, ch = conv_w.shape
    assert taps - 1 <= CONV_HALO
    nt = seq // tt
    ablk = a_col0 // ch
    per = tt // CONV_HALO

    def prev_map(cb):
        return lambda b, i: (jnp.maximum((b * nt + i) * per - 1, 0), cb)

    kernel = functools.partial(_conv_kernel, tt=tt, taps=taps, rows=rows, norm_rows=norm_rows)
    row = lambda a: a.reshape(1, ch)
    return pl.pallas_call(
        kernel,
        grid=(batch, nt),
        in_specs=[
            pl.BlockSpec((tt, ch), lambda b, i: (b * nt + i, ablk)),
            pl.BlockSpec((tt, ch), lambda b, i: (b * nt + i, ablk + 1)),
            pl.BlockSpec((CONV_HALO, ch), prev_map(ablk)),
            pl.BlockSpec((CONV_HALO, ch), prev_map(ablk + 1)),
            pl.BlockSpec((taps * SUBLANES, ch), lambda b, i: (0, 0)),
            pl.BlockSpec((1, ch), lambda b, i: (0, 0)),
            pl.BlockSpec((1, ch), lambda b, i: (0, 0)),
            pl.BlockSpec((1, ch), lambda b, i: (0, 0)),
        ],
        out_specs=pl.BlockSpec((tt, ch), lambda b, i: (b * nt + i, 0)),
        out_shape=jax.ShapeDtypeStruct((m, ch), BF16),
        scratch_shapes=[pltpu.VMEM((SUBLANES, tt + CONV_HALO, ch), F32), pltpu.VMEM((tt, ch), F32)],
        compiler_params=_cparams("parallel", "arbitrary"),
        name="conformer_conv",
    )(y, y, y, y, jnp.repeat(conv_w, SUBLANES, axis=0), row(conv_b), row(ln_g), row(ln_b))


def _outproj_kernel(a_ref, b_ref, wa_ref, wb_ref, x_ref, g_ref, o_ref, *, a_residue_major):
    a = a_ref[...]
    if a_residue_major:
        a = _dot(_residue_perm_matrix(a.shape[0], True), a).astype(BF16)
    y = _dot(a, wa_ref[...]) + _dot(b_ref[...], wb_ref[...])
    o_ref[...] = x_ref[...] + g_ref[...] * y


def out_proj(a, b, w_stack, layer, x2, ada3, g_idx, seq, a_residue_major=False, tm=512):
    m, d = x2.shape
    ka = a.shape[1]
    kb = b.shape[1]
    assert ka == kb
    assert not a_residue_major or tm == PERM_TILE
    tiles_per_batch = seq // tm
    return pl.pallas_call(
        functools.partial(_outproj_kernel, a_residue_major=a_residue_major),
        grid=(m // tm,),
        in_specs=[
            pl.BlockSpec((tm, ka), lambda i: (i, 0)),
            pl.BlockSpec((tm, kb), lambda i: (i, 0)),
            pl.BlockSpec((None, ka, d), lambda i: (layer, 0, 0)),
            pl.BlockSpec((None, kb, d), lambda i: (layer, 1, 0)),
            pl.BlockSpec((tm, d), lambda i: (i, 0)),
            pl.BlockSpec((None, 1, d), lambda i: (i // tiles_per_batch, 0, g_idx)),
        ],
        out_specs=pl.BlockSpec((tm, d), lambda i: (i, 0)),
        out_shape=jax.ShapeDtypeStruct((m, d), F32),
        compiler_params=_cparams("parallel"),
        name="out_proj",
    )(a, b, w_stack, w_stack, x2, ada3)


def _ffn_kernel(x_ref, g_ref, sc_ref, sh_ref, gate_ref, wg_ref, wu_ref, wd_ref, o_ref, h_scr, acc_scr):
    f = pl.program_id(1)

    @pl.when(f == 0)
    def _():
        h_scr[...] = _norm_mod(x_ref[...], g_ref[...], sc_ref[...], sh_ref[...]).astype(BF16)
        acc_scr[...] = jnp.zeros(acc_scr.shape, F32)

    h = h_scr[...]
    a = _silu(_dot(h, wg_ref[...])) * _dot(h, wu_ref[...])
    acc_scr[...] += _dot(a.astype(BF16), wd_ref[...])

    @pl.when(f == pl.num_programs(1) - 1)
    def _():
        o_ref[...] = x_ref[...] + gate_ref[...] * acc_scr[...]


def ffn_dense(x2, norm_g, ada3, sc_idx, sh_idx, g_idx, wg, wu, wd, layer, seq, tm=512, tf=512):
    m, d = x2.shape
    dff = wg.shape[2]
    tiles_per_batch = seq // tm
    mod = lambda k: pl.BlockSpec((None, 1, d), lambda i, f: (i // tiles_per_batch, 0, k))
    return pl.pallas_call(
        _ffn_kernel,
        grid=(m // tm, dff // tf),
        in_specs=[
            pl.BlockSpec((tm, d), lambda i, f: (i, 0)),
            pl.BlockSpec((1, d), lambda i, f: (0, 0)),
            mod(sc_idx), mod(sh_idx), mod(g_idx),
            pl.BlockSpec((None, d, tf), lambda i, f: (layer, 0, f)),
            pl.BlockSpec((None, d, tf), lambda i, f: (layer, 0, f)),
            pl.BlockSpec((None, tf, d), lambda i, f: (layer, f, 0)),
        ],
        out_specs=pl.BlockSpec((tm, d), lambda i, f: (i, 0)),
        out_shape=jax.ShapeDtypeStruct((m, d), F32),
        scratch_shapes=[pltpu.VMEM((tm, d), BF16), pltpu.VMEM((tm, d), F32)],
        compiler_params=_cparams("parallel", "arbitrary"),
        name="ffn_dense",
    )(x2, norm_g.reshape(1, d), ada3, ada3, ada3, wg, wu, wd)


def _router_kernel(x_ref, g_ref, sc_ref, sh_ref, rw_ref, cols_ref, rows_ref, counts_ref, *, n_experts):
    i = pl.program_id(0)

    @pl.when(i == 0)
    def _():
        counts_ref[...] = jnp.zeros(counts_ref.shape, F32)

    h = _norm_mod(x_ref[...], g_ref[...], sc_ref[...], sh_ref[...])
    rw = rw_ref[...]
    h_hi, rw_hi = h.astype(BF16), rw.astype(BF16)
    h_lo, rw_lo = (h - h_hi.astype(F32)).astype(BF16), (rw - rw_hi.astype(F32)).astype(BF16)
    logits = _dot(h_hi, rw_hi) + _dot(h_hi, rw_lo) + _dot(h_lo, rw_hi)
    tm = logits.shape[0]
    lane = lax.broadcasted_iota(I32, logits.shape, 1)
    lg = jnp.where(lane < n_experts, logits, NEG_INF)
    m1 = jnp.max(lg, axis=-1, keepdims=True)
    i1 = jnp.min(jnp.where(lg == m1, lane, LANES), axis=-1, keepdims=True)
    sel1 = lane == i1
    lg2 = jnp.where(sel1, NEG_INF, lg)
    m2 = jnp.max(lg2, axis=-1, keepdims=True)
    i2 = jnp.min(jnp.where(lg2 == m2, lane, LANES), axis=-1, keepdims=True)
    sel2 = lane == i2
    e = jnp.exp(m2 - m1)
    w1 = 1.0 / (1.0 + e)
    w2 = e / (1.0 + e)

    onehot = jnp.where(jnp.logical_or(sel1, sel2), 1.0, 0.0)
    tri = jnp.where(lax.broadcasted_iota(I32, (tm, tm), 0) > lax.broadcasted_iota(I32, (tm, tm), 1), 1.0, 0.0)
    rank = counts_ref[0:1, :] + _dot(tri.astype(BF16), onehot.astype(BF16))
    rank1 = jnp.sum(jnp.where(sel1, rank, 0.0), axis=-1, keepdims=True)
    rank2 = jnp.sum(jnp.where(sel2, rank, 0.0), axis=-1, keepdims=True)
    counts_ref[0:1, :] = counts_ref[0:1, :] + jnp.sum(onehot, axis=0, keepdims=True)

    cols = jnp.zeros(logits.shape, F32)
    for k, val in enumerate((rank1, rank2, i1.astype(F32), i2.astype(F32), w1, w2)):
        cols = jnp.where(lane == k, val, cols)
    cols_ref[...] = cols
    rows_ref[...] = jnp.transpose(cols)[0:SUBLANES, :].astype(I32)


def moe_router(x2, norm_g, ada3, sc_idx, sh_idx, router_w, seq, tm=512):
    m, d = x2.shape
    n_experts = router_w.shape[1]
    rw = jnp.zeros((d, LANES), F32).at[:, :n_experts].set(router_w)
    tiles_per_batch = seq // tm
    mod = lambda k: pl.BlockSpec((None, 1, d), lambda i: (i // tiles_per_batch, 0, k))
    return pl.pallas_call(
        functools.partial(_router_kernel, n_experts=n_experts),
        grid=(m // tm,),
        in_specs=[
            pl.BlockSpec((tm, d), lambda i: (i, 0)),
            pl.BlockSpec((1, d), lambda i: (0, 0)),
            mod(sc_idx), mod(sh_idx),
            pl.BlockSpec((d, LANES), lambda i: (0, 0)),
        ],
        out_specs=[
            pl.BlockSpec((tm, LANES), lambda i: (i, 0)),
            pl.BlockSpec((SUBLANES, tm), lambda i: (i, 0)),
            pl.BlockSpec((SUBLANES, LANES), lambda i: (0, 0)),
        ],
        out_shape=[
            jax.ShapeDtypeStruct((m, LANES), F32),
            jax.ShapeDtypeStruct((m // tm * SUBLANES, tm), I32),
            jax.ShapeDtypeStruct((SUBLANES, LANES), F32),
        ],
        compiler_params=_cparams("arbitrary"),
        name="moe_router",
    )(x2, norm_g.reshape(1, d), ada3, ada3, rw)


def _row_copy(src, src_row, dst, dst_row, sem):
    return pltpu.make_async_copy(src.at[pl.ds(src_row, 1)], dst.at[pl.ds(dst_row, 1)], sem)


def _dispatch_kernel(x_ref, g_ref, sc_ref, sh_ref, pos_ref, xs_in_ref, xs_ref, hp_scr, pos_smem, sem_pos, sem_rows):
    del xs_in_ref
    tm, half = hp_scr.shape
    pos_copy = pltpu.make_async_copy(pos_ref, pos_smem, sem_pos)
    pos_copy.start()
    h = _norm_mod(x_ref[...], g_ref[...], sc_ref[...], sh_ref[...])
    hp_scr[...] = _pack_bf16_pair(h[:, :half], h[:, half:])
    pos_copy.wait()

    def issue(t, carry):
        _row_copy(hp_scr, t, xs_ref, pos_smem[0, t], sem_rows).start()
        _row_copy(hp_scr, t, xs_ref, pos_smem[1, t], sem_rows).start()
        return carry

    lax.fori_loop(0, tm, issue, 0, unroll=ROW_DMA_UNROLL)

    def drain(t, carry):
        _row_copy(hp_scr, t, xs_ref, pos_smem[0, t], sem_rows).wait()
        _row_copy(hp_scr, t, xs_ref, pos_smem[1, t], sem_rows).wait()
        return carry

    lax.fori_loop(0, tm, drain, 0, unroll=ROW_DMA_UNROLL)


def moe_dispatch(x2, norm_g, ada3, sc_idx, sh_idx, pos_rows, n_rows, seq, tm=512):
    m, d = x2.shape
    half = d // 2
    tiles_per_batch = seq // tm
    mod = lambda k: pl.BlockSpec((None, 1, d), lambda i: (i // tiles_per_batch, 0, k))
    xs0 = jnp.zeros((n_rows, half), U32)
    return pl.pallas_call(
        _dispatch_kernel,
        grid=(m // tm,),
        in_specs=[
            pl.BlockSpec((tm, d), lambda i: (i, 0)),
            pl.BlockSpec((1, d), lambda i: (0, 0)),
            mod(sc_idx), mod(sh_idx),
            pl.BlockSpec((SUBLANES, tm), lambda i: (i, 0)),
            pl.BlockSpec(memory_space=pl.ANY),
        ],
        out_specs=pl.BlockSpec(memory_space=pl.ANY),
        out_shape=jax.ShapeDtypeStruct((n_rows, half), U32),
        scratch_shapes=[
            pltpu.VMEM((tm, half), U32), pltpu.SMEM((SUBLANES, tm), I32),
            pltpu.SemaphoreType.DMA, pltpu.SemaphoreType.DMA,
        ],
        input_output_aliases={5: 0},
        compiler_params=_cparams("arbitrary"),
        name="moe_dispatch",
    )(x2, norm_g.reshape(1, d), ada3, ada3, pos_rows, xs0)


def _moe_ffn_kernel(te_ref, tv_ref, xs_ref, wg_ref, wu_ref, wd_ref, ys_ref, h_scr, acc_scr):
    del te_ref
    g = pl.program_id(0)
    f = pl.program_id(1)
    half = xs_ref.shape[1]
    used = tv_ref[g] > 0
    last = f == pl.num_programs(1) - 1

    @pl.when(jnp.logical_and(used, f == 0))
    def _():
        lo, hi = _unpack_bf16_pair(xs_ref[...])
        h_scr[:, :half] = lo.astype(BF16)
        h_scr[:, half:] = hi.astype(BF16)
        acc_scr[...] = jnp.zeros(acc_scr.shape, F32)

    @pl.when(used)
    def _():
        h = h_scr[...]
        a = _silu(_dot(h, wg_ref[...])) * _dot(h, wu_ref[...])
        acc_scr[...] += _dot(a.astype(BF16), wd_ref[...])

    @pl.when(jnp.logical_and(used, last))
    def _():
        ys_ref[...] = _pack_bf16_pair(acc_scr[:, :half], acc_scr[:, half:])

    @pl.when(jnp.logical_and(jnp.logical_not(used), last))
    def _():
        ys_ref[...] = jnp.zeros(ys_ref.shape, U32)


def moe_expert_ffn(xs, tile_expert, tile_used, wg, wu, wd, layer, tm, tf=512):
    n_rows, half = xs.shape
    d = 2 * half
    dff = wg.shape[3]
    grid_spec = pltpu.PrefetchScalarGridSpec(
        num_scalar_prefetch=2,
        grid=(n_rows // tm, dff // tf),
        in_specs=[
            pl.BlockSpec((tm, half), lambda g, f, te, tv: (g, 0)),
            pl.BlockSpec((None, None, d, tf), lambda g, f, te, tv: (layer, te[g], 0, f * tv[g])),
            pl.BlockSpec((None, None, d, tf), lambda g, f, te, tv: (layer, te[g], 0, f * tv[g])),
            pl.BlockSpec((None, None, tf, d), lambda g, f, te, tv: (layer, te[g], f * tv[g], 0)),
        ],
        out_specs=pl.BlockSpec((tm, half), lambda g, f, te, tv: (g, 0)),
        scratch_shapes=[pltpu.VMEM((tm, d), BF16), pltpu.VMEM((tm, d), F32)],
    )
    return pl.pallas_call(
        _moe_ffn_kernel,
        grid_spec=grid_spec,
        out_shape=jax.ShapeDtypeStruct((n_rows, half), U32),
        compiler_params=_cparams("parallel", "arbitrary"),
        name="moe_expert_ffn",
    )(tile_expert, tile_used, xs, wg, wu, wd)


def _combine_kernel(x_ref, gate_ref, cols_ref, pos_ref, ys_ref, o_ref, r1_scr, r2_scr, pos_smem, sem_pos, sem_rows):
    tm, half = r1_scr.shape
    pos_copy = pltpu.make_async_copy(pos_ref, pos_smem, sem_pos)
    pos_copy.start()
    pos_copy.wait()

    def issue(t, carry):
        _row_copy(ys_ref, pos_smem[0, t], r1_scr, t, sem_rows).start()
        _row_copy(ys_ref, pos_smem[1, t], r2_scr, t, sem_rows).start()
        return carry

    lax.fori_loop(0, tm, issue, 0, unroll=ROW_DMA_UNROLL)

    def drain(t, carry):
        _row_copy(ys_ref, pos_smem[0, t], r1_scr, t, sem_rows).wait()
        _row_copy(ys_ref, pos_smem[1, t], r2_scr, t, sem_rows).wait()
        return carry

    lax.fori_loop(0, tm, drain, 0, unroll=ROW_DMA_UNROLL)

    w1 = cols_ref[:, 4:5]
    w2 = cols_ref[:, 5:6]
    lo1, hi1 = _unpack_bf16_pair(r1_scr[...])
    lo2, hi2 = _unpack_bf16_pair(r2_scr[...])
    o_ref[:, :half] = x_ref[:, :half] + gate_ref[:, :half] * (w1 * lo1 + w2 * lo2)
    o_ref[:, half:] = x_ref[:, half:] + gate_ref[:, half:] * (w1 * hi1 + w2 * hi2)


def moe_combine(x2, ada3, g_idx, cols, pos_rows, ys, seq, tm=512):
    m, d = x2.shape
    half = d // 2
    tiles_per_batch = seq // tm
    return pl.pallas_call(
        _combine_kernel,
        grid=(m // tm,),
        in_specs=[
            pl.BlockSpec((tm, d), lambda i: (i, 0)),
            pl.BlockSpec((None, 1, d), lambda i: (i // tiles_per_batch, 0, g_idx)),
            pl.BlockSpec((tm, LANES), lambda i: (i, 0)),
            pl.BlockSpec((SUBLANES, tm), lambda i: (i, 0)),
            pl.BlockSpec(memory_space=pl.ANY),
        ],
        out_specs=pl.BlockSpec((tm, d), lambda i: (i, 0)),
        out_shape=jax.ShapeDtypeStruct((m, d), F32),
        scratch_shapes=[
            pltpu.VMEM((tm, half), U32), pltpu.VMEM((tm, half), U32), pltpu.SMEM((SUBLANES, tm), I32),
            pltpu.SemaphoreType.DMA, pltpu.SemaphoreType.DMA,
        ],
        compiler_params=_cparams("arbitrary"),
        name="moe_combine",
    )(x2, ada3, cols, pos_rows, ys)


def moe_ffn(x2, norm_g, ada3, sc_idx, sh_idx, g_idx, router_w, wg, wu, wd, layer, seq, tm=512,
            row_tile=MOE_ROW_TILE):
    m = x2.shape[0]
    n_experts = router_w.shape[1]
    row_tile = min(row_tile, m)
    cols, rows, counts = moe_router(x2, norm_g, ada3, sc_idx, sh_idx, router_w, seq, tm=tm)
    n_rows = 2 * m + n_experts * row_tile
    counts = counts[0, :n_experts].astype(I32)
    padded = ((counts + row_tile - 1) // row_tile) * row_tile
    ends = jnp.cumsum(padded)
    offs = ends - padded
    rows3 = rows.reshape(m // tm, SUBLANES, tm)
    pos_rows = rows3.at[:, 0].add(offs[rows3[:, 2]]).at[:, 1].add(offs[rows3[:, 3]]).reshape(rows.shape)
    tile_start = jnp.arange(n_rows // row_tile, dtype=I32) * row_tile
    tile_expert = jnp.minimum(jnp.sum(tile_start[:, None] >= ends[None, :], axis=1), n_experts - 1).astype(I32)
    tile_used = (tile_start < ends[-1]).astype(I32)

    xs = moe_dispatch(x2, norm_g, ada3, sc_idx, sh_idx, pos_rows, n_rows, seq, tm=tm)
    ys = moe_expert_ffn(xs, tile_expert, tile_used, wg, wu, wd, layer, row_tile)
    return moe_combine(x2, ada3, g_idx, cols, pos_rows, ys, seq, tm=tm)


def _head_gain(q_g, k_g, n_q_heads, n_k_heads, q_scale):
    return jnp.concatenate([jnp.tile(q_g * q_scale, n_q_heads), jnp.tile(k_g, n_k_heads)]).reshape(1, -1).astype(F32)


def kernel(x, c, ada_w, ada_b, norm_mix_g, norm_ffn_g, ev_w_in, ev_w_out, a_q_norm_g, a_k_norm_g, a_lambda,
           a_head_norm_g, ffn_w_gate, ffn_w_up, ffn_w_down, od_w_in, od_w_out, c_q_norm_g, c_k_norm_g,
           d_conv_w, d_conv_b, d_ln_g, d_ln_b, moe_router_w, moe_w_gate, moe_w_up, moe_w_down):
    batch, seq, d = x.shape
    depth = ada_w.shape[0]
    m = batch * seq
    scale = HEAD_DIM ** -0.5
    a_width = ev_w_out.shape[1] // 2
    a_heads = a_width // (2 * HEAD_DIM)
    b_heads = a_width // HEAD_DIM
    c_width = od_w_out.shape[1] // 2
    c_heads = c_width // HEAD_DIM

    ev_w_in, ev_w_out, od_w_in, od_w_out = (w.astype(BF16) for w in (ev_w_in, ev_w_out, od_w_in, od_w_out))
    ffn_w = tuple(w.astype(BF16) for w in (ffn_w_gate, ffn_w_up, ffn_w_down))
    moe_w = tuple(w.astype(BF16) for w in (moe_w_gate, moe_w_up, moe_w_down))

    ada = ada_all(c, ada_w, ada_b)
    x2 = x.reshape(m, d)
    for i in range(depth):
        j = i // 2
        ada3 = ada[i].reshape(batch, 1, 6 * d)
        if i % 2 == 0:
            lam_init = 0.8 - 0.6 * math.exp(-0.3 * i)
            lv = a_lambda[j].astype(F32)
            lam = jnp.exp(jnp.sum(lv[0] * lv[1])) - jnp.exp(jnp.sum(lv[2] * lv[3])) + lam_init
            slopes = 2.0 ** (-8.0 * jnp.arange(1, a_heads + 1, dtype=F32) / a_heads)
            gain = _head_gain(a_q_norm_g[j], a_k_norm_g[j], 2 * a_heads, 2 * a_heads, scale * LOG2E)
            (y,) = in_proj(x2, norm_mix_g[i], ada3, 1, 0, ev_w_in, j, gain, seq)
            oa = diff_attention(y, slopes, lam.reshape(1), a_head_norm_g[j], batch, seq, a_heads,
                                k_col0=a_width, v_col0=2 * a_width, out_scale=1.0 - lam_init)
            ob = stick_attention(y, batch, seq, b_heads, q_col0=3 * a_width, k_col0=4 * a_width,
                                 v_col0=5 * a_width)
            x2 = out_proj(oa, ob, ev_w_out, j, x2, ada3, 2, seq)
            x2 = ffn_dense(x2, norm_ffn_g[i], ada3, 4, 3, 5, *ffn_w, j, seq)
        else:
            gain = _head_gain(c_q_norm_g[j], c_k_norm_g[j], c_heads, c_heads, scale)
            qkv, du = in_proj(x2, norm_mix_g[i], ada3, 1, 0, od_w_in, j, gain, seq, n_first=3 * c_width,
                              perm_first=True)
            oc = dilated_mixture(qkv, batch, seq, c_heads)
            od = conformer_conv(du, d_conv_w[j], d_conv_b[j], d_ln_g[j], d_ln_b[j], batch, seq, a_col0=0)
            x2 = out_proj(oc, od, od_w_out, j, x2, ada3, 2, seq, a_residue_major=True)
            x2 = moe_ffn(x2, norm_ffn_g[i], ada3, 4, 3, 5, moe_router_w[j], *moe_w, j, seq)
    return x2.reshape(batch, seq, d)
```

```python
import functools
import math

import jax
import jax.numpy as jnp
from jax import lax
from jax.experimental import pallas as pl
from jax.experimental.pallas import tpu as pltpu

F32 = jnp.float32
BF16 = jnp.bfloat16
U32 = jnp.uint32
I32 = jnp.int32
EPS = 1e-6
HEAD_DIM = 128
LANES = 128
SUBLANES = 8
VMEM_LIMIT_BYTES = 56 * 1024 * 1024
CONV_HALO = 32
STICK_UNDERFLOW = -104.0
DILATED = ((128, 1), (512, 4), (2048, 16))
NEG_INF = float("-inf")
LOG2E = math.log2(math.e)
MOE_ROW_TILE = 1024
ROW_DMA_UNROLL = 8
RESIDUES = 16
PERM_TILE = 512
BAND_STEPS = 128


def _cparams(*sem):
    return pltpu.CompilerParams(dimension_semantics=sem, vmem_limit_bytes=VMEM_LIMIT_BYTES)


def _dot(a, b):
    return jnp.dot(a, b, preferred_element_type=F32)


def _dot_nt(a, b):
    return lax.dot_general(a, b, (((1,), (1,)), ((), ())), preferred_element_type=F32)


def _silu(x):
    return x * (1.0 / (1.0 + jnp.exp(-x)))


def _norm_mod(x, g, sc, sh):
    ms = jnp.mean(x * x, axis=-1, keepdims=True)
    return (x * lax.rsqrt(ms + EPS) * g) * (1.0 + sc) + sh


def _pack_bf16_pair(lo, hi):
    lo_bits = lax.bitcast_convert_type(lo.astype(BF16).astype(F32), U32)
    hi_bits = lax.bitcast_convert_type(hi.astype(BF16).astype(F32), U32)
    return hi_bits | (lo_bits >> 16)


def _unpack_bf16_pair(w):
    lo = lax.bitcast_convert_type(w << 16, F32)
    hi = lax.bitcast_convert_type(w & U32(0xFFFF0000), F32)
    return lo, hi


def _ada_kernel(c_ref, w_ref, b_ref, o_ref):
    cond = _silu(c_ref[...])
    o_ref[...] = jnp.dot(cond, w_ref[...], preferred_element_type=F32,
                         precision=lax.Precision.HIGHEST) + b_ref[...]


def ada_all(c, ada_w, ada_b, tn=1024):
    depth, d, n = ada_w.shape
    b = c.shape[0]
    return pl.pallas_call(
        _ada_kernel,
        grid=(depth, n // tn),
        in_specs=[
            pl.BlockSpec((b, d), lambda i, j: (0, 0)),
            pl.BlockSpec((None, d, tn), lambda i, j: (i, 0, j)),
            pl.BlockSpec((None, 1, tn), lambda i, j: (i, 0, j)),
        ],
        out_specs=pl.BlockSpec((None, b, tn), lambda i, j: (i, 0, j)),
        out_shape=jax.ShapeDtypeStruct((depth, b, n), F32),
        compiler_params=_cparams("parallel", "parallel"),
        name="ada",
    )(c, ada_w, ada_b.reshape(depth, 1, n))


def _residue_perm_matrix(n, inverse):
    c = n // RESIDUES
    i = lax.broadcasted_iota(I32, (n, n), 1 if inverse else 0)
    t = lax.broadcasted_iota(I32, (n, n), 0 if inverse else 1)
    return jnp.where(t == _interleave_pos(i, c, RESIDUES), 1.0, 0.0).astype(BF16)


def _interleave_pos(i, inner, outer):
    shift = inner.bit_length() - 1
    assert inner == 1 << shift
    return (i & (inner - 1)) * outer + (i >> shift)


def _inproj_kernel(x_ref, g_ref, sc_ref, sh_ref, w_ref, hg_ref, *rest, n_out, n_norm_tiles, n_first_tiles,
                   perm_first):
    o_refs, scr = rest[:n_out], rest[n_out:]
    h_scr = scr[0]
    first_lhs = scr[1] if perm_first else h_scr
    j = pl.program_id(1)

    @pl.when(j == 0)
    def _():
        h = _norm_mod(x_ref[...], g_ref[...], sc_ref[...], sh_ref[...]).astype(BF16)
        h_scr[...] = h
        if perm_first:
            first_lhs[...] = _dot(_residue_perm_matrix(h.shape[0], False), h).astype(BF16)

    @pl.when(j < n_first_tiles)
    def _():
        y = _dot(first_lhs[...], w_ref[...])
        tn = y.shape[1]

        @pl.when(j < n_norm_tiles)
        def _():
            for c in range(tn // HEAD_DIM):
                sl = slice(c * HEAD_DIM, (c + 1) * HEAD_DIM)
                blk = y[:, sl]
                r = lax.rsqrt(jnp.mean(blk * blk, axis=-1, keepdims=True) + EPS)
                o_refs[0][:, sl] = (blk * r * hg_ref[:, sl]).astype(BF16)

        @pl.when(j >= n_norm_tiles)
        def _():
            o_refs[0][...] = y.astype(BF16)

    if n_out > 1:
        @pl.when(j >= n_first_tiles)
        def _():
            o_refs[1][...] = _dot(h_scr[...], w_ref[...]).astype(BF16)


def in_proj(x2, norm_g, ada3, sc_idx, sh_idx, w_stack, layer, head_gain, seq, n_first=None, perm_first=False,
            tm=512, tn=1024):
    m, d = x2.shape
    n = w_stack.shape[2]
    n_first = n if n_first is None else n_first
    n_norm_tiles = head_gain.shape[1] // tn
    n_first_tiles = n_first // tn
    assert n_norm_tiles <= n_first_tiles
    assert not perm_first or tm == PERM_TILE
    tiles_per_batch = seq // tm
    out_specs = [pl.BlockSpec((tm, tn), lambda i, j: (i, jnp.minimum(j, n_first_tiles - 1)))]
    out_shape = [jax.ShapeDtypeStruct((m, n_first), BF16)]
    if n_first < n:
        out_specs.append(pl.BlockSpec((tm, tn), lambda i, j: (i, jnp.maximum(j - n_first_tiles, 0))))
        out_shape.append(jax.ShapeDtypeStruct((m, n - n_first), BF16))
    return pl.pallas_call(
        functools.partial(_inproj_kernel, n_out=len(out_specs), n_norm_tiles=n_norm_tiles,
                          n_first_tiles=n_first_tiles, perm_first=perm_first),
        grid=(m // tm, n // tn),
        in_specs=[
            pl.BlockSpec((tm, d), lambda i, j: (i, 0)),
            pl.BlockSpec((1, d), lambda i, j: (0, 0)),
            pl.BlockSpec((None, 1, d), lambda i, j: (i // tiles_per_batch, 0, sc_idx)),
            pl.BlockSpec((None, 1, d), lambda i, j: (i // tiles_per_batch, 0, sh_idx)),
            pl.BlockSpec((None, d, tn), lambda i, j: (layer, 0, j)),
            pl.BlockSpec((1, tn), lambda i, j: (0, jnp.minimum(j, n_norm_tiles - 1))),
        ],
        out_specs=out_specs,
        out_shape=out_shape,
        scratch_shapes=[pltpu.VMEM((tm, d), BF16)] * (2 if perm_first else 1),
        compiler_params=_cparams("parallel", "arbitrary"),
        name="in_proj",
    )(x2, norm_g.reshape(1, d), ada3, ada3, w_stack, head_gain)


def _diff_attn_kernel(slopes_ref, lam_ref, q_ref, k_ref, v_ref, hg_ref, o_ref, m_scr, l_scr, a_scr,
                      *, tq, out_scale, heads):
    hg0 = pl.program_id(1) * heads
    qi = pl.program_id(2)
    hw = 2 * HEAD_DIM
    lam = lam_ref[0]
    col = lax.broadcasted_iota(I32, (1, tq), 1).astype(F32)
    reps = tq // LANES
    maps = [(h, a) for h in range(heads) for a in range(2)]
    qsl = [slice(h * hw + a * HEAD_DIM, h * hw + (a + 1) * HEAD_DIM) for h, a in maps]
    vsl = [slice(h * hw, (h + 1) * hw) for h, _ in maps]

    m_scr[...] = jnp.full(m_scr.shape, NEG_INF, F32)
    l_scr[...] = jnp.zeros(l_scr.shape, F32)
    a_scr[...] = jnp.zeros(a_scr.shape, F32)

    def block(kb, masked):
        ks = pl.multiple_of(kb * tq, tq)
        kpos = col + ((kb - qi) * tq).astype(F32)
        cbias = [slopes_ref[hg0 + h] * LOG2E * kpos for h in range(heads)]
        ss = [_dot_nt(q_ref[:, sl], k_ref[pl.ds(ks, tq), sl]) + cbias[h] for sl, (h, _) in zip(qsl, maps)]
        if masked:
            rc = lax.broadcasted_iota(I32, (tq, tq), 0) - lax.broadcasted_iota(I32, (tq, tq), 1)
            ss = [jnp.where(rc >= 0, s, NEG_INF) for s in ss]
        m_olds = [m_scr[i] for i in range(len(maps))]
        m_news = [jnp.maximum(m_old, jnp.max(s, axis=-1, keepdims=True)) for m_old, s in zip(m_olds, ss)]
        ps = [jnp.exp2(s - jnp.tile(m_new, (1, reps))) for s, m_new in zip(ss, m_news)]
        pvs = [_dot(p.astype(BF16), v_ref[pl.ds(ks, tq), sl]) for p, sl in zip(ps, vsl)]
        for i in range(len(maps)):
            alpha = jnp.exp2(m_olds[i] - m_news[i])
            l_scr[i] = alpha * l_scr[i] + jnp.sum(ps[i], axis=-1, keepdims=True)
            a_scr[i] = jnp.tile(alpha, (1, 2)) * a_scr[i] + pvs[i]
            m_scr[i] = m_news[i]

    def body(kb, carry):
        block(kb, False)
        return carry

    lax.fori_loop(0, qi, body, 0)
    block(qi, True)

    for h in range(heads):
        inv1 = jnp.tile(1.0 / l_scr[2 * h], (1, 2))
        inv2 = jnp.tile(1.0 / l_scr[2 * h + 1], (1, 2))
        o = a_scr[2 * h] * inv1 - lam * (a_scr[2 * h + 1] * inv2)
        r = lax.rsqrt(jnp.mean(o * o, axis=-1, keepdims=True) + EPS)
        o_ref[:, h * hw:(h + 1) * hw] = (o * r * hg_ref[...] * out_scale).astype(BF16)


def diff_attention(y, slopes, lam, head_g, batch, seq, n_heads, k_col0, v_col0, out_scale, tq=512, heads=2):
    m = y.shape[0]
    hw = 2 * HEAD_DIM
    bw = heads * hw
    nq = seq // tq
    kernel = functools.partial(_diff_attn_kernel, tq=tq, out_scale=out_scale, heads=heads)
    return pl.pallas_call(
        kernel,
        grid=(batch, n_heads // heads, nq),
        in_specs=[
            pl.BlockSpec(memory_space=pltpu.SMEM),
            pl.BlockSpec(memory_space=pltpu.SMEM),
            pl.BlockSpec((tq, bw), lambda b, h, qi: (b * nq + qi, h)),
            pl.BlockSpec((seq, bw), lambda b, h, qi: (b, k_col0 // bw + h)),
            pl.BlockSpec((seq, bw), lambda b, h, qi: (b, v_col0 // bw + h)),
            pl.BlockSpec((1, hw), lambda b, h, qi: (0, 0)),
        ],
        out_specs=pl.BlockSpec((tq, bw), lambda b, h, qi: (b * nq + qi, h)),
        out_shape=jax.ShapeDtypeStruct((m, n_heads * hw), BF16),
        scratch_shapes=[
            pltpu.VMEM((2 * heads, tq, LANES), F32), pltpu.VMEM((2 * heads, tq, LANES), F32),
            pltpu.VMEM((2 * heads, tq, hw), F32),
        ],
        compiler_params=_cparams("parallel", "parallel", "arbitrary"),
        name="diff_attn",
    )(slopes, lam, y, y, y, head_g.reshape(1, hw))


def _stick_kernel(q_ref, k_ref, v_ref, o_ref, *scratch, tq, tk, scale, heads):
    acc_scrs, carry_scrs = scratch[:heads], scratch[heads:]
    qi = pl.program_id(2)
    rc = lax.broadcasted_iota(I32, (tq, tk), 0) - lax.broadcasted_iota(I32, (tq, tk), 1)
    jj = lax.broadcasted_iota(I32, (tk, 2 * tk), 0)
    ss = lax.broadcasted_iota(I32, (tk, 2 * tk), 1)
    suffix = jnp.where(jnp.logical_or(jj > ss, ss >= tk), 1.0, 0.0).astype(BF16)

    for ref in scratch:
        ref[...] = jnp.zeros(ref.shape, F32)
    kb0 = (qi * tq + tq) // tk - 1

    def cond(state):
        kb, live = state
        return jnp.logical_and(kb >= 0, live > STICK_UNDERFLOW)

    def body(state):
        kb, _ = state
        ks = pl.multiple_of(kb * tk, tk)
        past = rc + (qi * tq - kb * tk) > 0
        sls = [slice(h * HEAD_DIM, (h + 1) * HEAD_DIM) for h in range(heads)]
        zs = [_dot_nt((q_ref[:, sl].astype(F32) * scale).astype(BF16), k_ref[pl.ds(ks, tk), sl]) for sl in sls]
        lsigs = [jnp.minimum(z, 0.0) - jnp.log(1.0 + jnp.exp(-jnp.abs(z))) for z in zs]
        l1ms = [jnp.where(past, lsig - z, 0.0) for lsig, z in zip(lsigs, zs)]
        his = [l1m.astype(BF16) for l1m in l1ms]
        los = [(l1m - hi.astype(F32)).astype(BF16) for l1m, hi in zip(l1ms, his)]
        sums = [_dot(hi, suffix) + _dot(lo, suffix) for hi, lo in zip(his, los)]
        carries = [ref[...] for ref in carry_scrs]
        ws = [jnp.where(past, jnp.exp(lsig + carry + sm[:, :tk]), 0.0).astype(BF16)
              for lsig, carry, sm in zip(lsigs, carries, sums)]
        pvs = [_dot(w, v_ref[pl.ds(ks, tk), sl]) for w, sl in zip(ws, sls)]
        live = jnp.full((tq, tk), NEG_INF, F32)
        for h in range(heads):
            acc_scrs[h][...] += pvs[h]
            carry = carries[h] + sums[h][:, tk:]
            carry_scrs[h][...] = carry
            live = jnp.maximum(live, carry)
        return kb - 1, jnp.max(live)

    lax.while_loop(cond, body, (kb0, jnp.float32(0.0)))
    for h in range(heads):
        o_ref[:, h * HEAD_DIM:(h + 1) * HEAD_DIM] = acc_scrs[h][...].astype(BF16)


def stick_attention(y, batch, seq, n_heads, q_col0, k_col0, v_col0, tq=256, tk=128, heads=4):
    m = y.shape[0]
    nq = seq // tq
    bw = heads * HEAD_DIM
    assert tk == LANES
    kernel = functools.partial(_stick_kernel, tq=tq, tk=tk, scale=HEAD_DIM ** -0.5, heads=heads)
    return pl.pallas_call(
        kernel,
        grid=(batch, n_heads // heads, nq),
        in_specs=[
            pl.BlockSpec((tq, bw), lambda b, h, qi: (b * nq + qi, q_col0 // bw + h)),
            pl.BlockSpec((seq, bw), lambda b, h, qi: (b, k_col0 // bw + h)),
            pl.BlockSpec((seq, bw), lambda b, h, qi: (b, v_col0 // bw + h)),
        ],
        out_specs=pl.BlockSpec((tq, bw), lambda b, h, qi: (b * nq + qi, h)),
        out_shape=jax.ShapeDtypeStruct((m, n_heads * HEAD_DIM), BF16),
        scratch_shapes=[pltpu.VMEM((tq, HEAD_DIM), F32)] * heads + [pltpu.VMEM((tq, tk), F32)] * heads,
        compiler_params=_cparams("parallel", "parallel", "arbitrary"),
        name="stick_attn",
    )(y, y, y)


def _dilated_kernel(*refs, n_heads, dilation, has_state, final, qb, pos_inner, pos_outer):
    if has_state:
        q_ref, kc_ref, kp_ref, vc_ref, vp_ref, acc_in_ref, m_in_ref, l_in_ref = refs[:8]
        outs = refs[8:]
    else:
        q_ref, kc_ref, kp_ref, vc_ref, vp_ref = refs[:5]
        outs = refs[5:]
    nb = pl.program_id(2)
    blk_shape = q_ref.shape

    def rows2d(ref):
        return ref[...].reshape(qb, ref.shape[-1])

    q, kc, kp, vc, vp = (rows2d(r) for r in (q_ref, kc_ref, kp_ref, vc_ref, vp_ref))
    row = lax.broadcasted_iota(I32, (qb, qb), 0)
    col = lax.broadcasted_iota(I32, (qb, qb), 1)
    if pos_outer > 1:
        row, col = _interleave_pos(row, pos_inner, pos_outer), _interleave_pos(col, pos_inner, pos_outer)
    steps_c = row - col
    steps_p = steps_c + qb
    in_band_c = jnp.abs(steps_c - BAND_STEPS // 2) <= BAND_STEPS // 2
    in_band_p = steps_p <= jnp.where(nb > 0, BAND_STEPS, 0)
    dist_c = steps_c.astype(F32)
    dist_p = steps_p.astype(F32)
    reps = qb // HEAD_DIM

    sls = [slice(h * HEAD_DIM, (h + 1) * HEAD_DIM) for h in range(n_heads)]
    slopes = [2.0 ** (-8.0 * (h + 1) / n_heads) * dilation for h in range(n_heads)]
    s_cs = [_dot_nt(q[:, sl], kc[:, sl]) + jnp.where(in_band_c, -slope * dist_c, NEG_INF)
            for sl, slope in zip(sls, slopes)]
    s_ps = [_dot_nt(q[:, sl], kp[:, sl]) + jnp.where(in_band_p, -slope * dist_p, NEG_INF)
            for sl, slope in zip(sls, slopes)]
    m_blks = [jnp.maximum(jnp.max(s_c, axis=-1, keepdims=True), jnp.max(s_p, axis=-1, keepdims=True))
              for s_c, s_p in zip(s_cs, s_ps)]
    if has_state:
        acc_in, m_in, l_in = rows2d(acc_in_ref), rows2d(m_in_ref), rows2d(l_in_ref)
        m_news = [jnp.maximum(m_in[:, sl], m) for m, sl in zip(m_blks, sls)]
    else:
        m_news = [jnp.broadcast_to(m, (qb, HEAD_DIM)) for m in m_blks]
    m_wide = [jnp.tile(m, (1, reps)) if reps > 1 else m for m in m_news]
    p_cs = [jnp.exp(s_c - m) for s_c, m in zip(s_cs, m_wide)]
    p_ps = [jnp.exp(s_p - m) for s_p, m in zip(s_ps, m_wide)]
    accs = [_dot(p_c.astype(BF16), vc[:, sl]) + _dot(p_p.astype(BF16), vp[:, sl])
            for p_c, p_p, sl in zip(p_cs, p_ps, sls)]
    acc_cols, m_cols, l_cols = [], [], []
    for h in range(n_heads):
        l = jnp.sum(p_cs[h], axis=-1, keepdims=True) + jnp.sum(p_ps[h], axis=-1, keepdims=True)
        acc = accs[h]
        if has_state:
            r_in = jnp.exp(m_in[:, sls[h]] - m_news[h])
            l = l + l_in[:, sls[h]] * r_in
            acc = acc + acc_in[:, sls[h]] * r_in
        else:
            l = jnp.broadcast_to(l, (qb, HEAD_DIM))
        if final:
            acc_cols.append((acc / l).astype(BF16))
        else:
            acc_cols.append(acc)
            m_cols.append(m_news[h])
            l_cols.append(l)
    outs[0][...] = jnp.concatenate(acc_cols, axis=1).reshape(blk_shape)
    if not final:
        outs[1][...] = jnp.concatenate(m_cols, axis=1).reshape(blk_shape)
        outs[2][...] = jnp.concatenate(l_cols, axis=1).reshape(blk_shape)


def dilated_branch(qkv, state, batch, seq, n_heads, dilation, final):
    m, n = qkv.shape
    hw = n_heads * HEAD_DIM
    tiles = m // PERM_TILE
    tpb = seq // PERM_TILE
    rows = PERM_TILE // RESIDUES
    if dilation == RESIDUES:
        span = BAND_STEPS // rows
        assert tpb % span == 0
        qb, pos = BAND_STEPS, (1, 1)
        grid = (batch, RESIDUES, tpb // span)
        view = lambda a: a.reshape(tiles, RESIDUES, rows, a.shape[-1])
        blk = lambda w: (span, None, rows, w)
        idx = lambda b, r, i, g: (b * (tpb // span) + i, r, 0, g)
    elif dilation * BAND_STEPS == PERM_TILE:
        sub = RESIDUES // dilation
        qb, pos = BAND_STEPS, (rows, sub)
        grid = (batch, dilation, tpb)
        view = lambda a: a.reshape(tiles, sub, dilation, rows, a.shape[-1])
        blk = lambda w: (None, sub, None, rows, w)
        idx = lambda b, r, i, g: (b * tpb + i, 0, r, 0, g)
    else:
        assert dilation == 1
        qb = 2 * BAND_STEPS
        sub = qb // RESIDUES
        per_tile = rows // sub
        qb, pos = qb, (sub, RESIDUES)
        grid = (batch, 1, tpb * per_tile)
        view = lambda a: a.reshape(tiles, RESIDUES, rows, a.shape[-1])
        blk = lambda w: (None, RESIDUES, sub, w)
        idx = lambda b, r, i, g: (b * tpb + i // per_tile, 0, i % per_tile, g)

    def spec(w, g, back):
        return pl.BlockSpec(blk(w), lambda b, r, i: idx(b, r, jnp.maximum(i - back, 0), g))

    qv = view(qkv)
    in_specs = [spec(hw, 0, 0), spec(hw, 1, 0), spec(hw, 1, 1), spec(hw, 2, 0), spec(hw, 2, 1)]
    args = [qv] * 5
    if state is not None:
        in_specs += [spec(hw, 0, 0)] * 3
        args += [view(a) for a in state]

    def shape(dt):
        return jax.ShapeDtypeStruct(jax.eval_shape(view, jax.ShapeDtypeStruct((m, hw), dt)).shape, dt)

    if final:
        out_specs = [spec(hw, 0, 0)]
        out_shape = [shape(BF16)]
    else:
        out_specs = [spec(hw, 0, 0)] * 3
        out_shape = [shape(F32)] * 3
    kernel = functools.partial(_dilated_kernel, n_heads=n_heads, dilation=dilation, has_state=state is not None,
                               final=final, qb=qb, pos_inner=pos[0], pos_outer=pos[1])
    outs = pl.pallas_call(
        kernel, grid=grid, in_specs=in_specs, out_specs=out_specs, out_shape=out_shape,
        compiler_params=_cparams("parallel", "parallel", "arbitrary"),
        name=f"dilated_d{dilation}",
    )(*args)
    if final:
        return outs[0].reshape(m, hw)
    return tuple(o.reshape(m, hw) for o in outs)


def dilated_mixture(qkv, batch, seq, n_heads):
    state = None
    branches = sorted(DILATED, key=lambda wd: -wd[1])
    for idx, (window, dilation) in enumerate(branches):
        assert window // dilation == BAND_STEPS
        state = dilated_branch(qkv, state, batch, seq, n_heads, dilation, final=idx == len(branches) - 1)
    return state


def _conv_kernel(a_ref, g_ref, ap_ref, gp_ref, w_ref, cb_ref, lg_ref, lb_ref, o_ref, hs_scr, y_scr,
                 *, tt, taps, rows, norm_rows):
    i = pl.program_id(1)
    halo = hs_scr.shape[1] - tt
    n_ch = hs_scr.shape[2]
    off = halo - (taps - 1)

    def glu(a, g):
        return a.astype(F32) * (1.0 / (1.0 + jnp.exp(-g.astype(F32))))

    hs_scr[0, 0:halo, :] = jnp.where(i > 0, glu(ap_ref[...], gp_ref[...]), 0.0)
    hs_scr[0, halo:, :] = glu(a_ref[...], g_ref[...])

    taps_of = {b: [a for a in range(halo // SUBLANES + 1) if 0 <= SUBLANES * a + b - off < taps]
               for b in range(SUBLANES)}
    for b in range(1, SUBLANES):
        if taps_of[b]:
            span = tt + SUBLANES * max(taps_of[b])
            hs_scr[b, 0:span, :] = hs_scr[0, b:b + span, :]

    def conv_chunk(c, carry):
        r0 = pl.multiple_of(c * rows, rows)
        strips = [slice(s * LANES, (s + 1) * LANES) for s in range(n_ch // LANES)]
        accs = [None] * len(strips)
        for b in range(SUBLANES):
            for a in taps_of[b]:
                j = SUBLANES * a + b - off
                for s, ls in enumerate(strips):
                    w_tap = jnp.tile(w_ref[SUBLANES * j:SUBLANES * (j + 1), ls], (rows // SUBLANES, 1))
                    term = hs_scr[b, pl.ds(r0 + SUBLANES * a, rows), ls] * w_tap
                    accs[s] = term if accs[s] is None else accs[s] + term
        for s, ls in enumerate(strips):
            y_scr[pl.ds(r0, rows), ls] = accs[s]
        return carry

    lax.fori_loop(0, tt // rows, conv_chunk, 0)

    def norm_chunk(c, carry):
        r0 = pl.multiple_of(c * norm_rows, norm_rows)
        y = y_scr[pl.ds(r0, norm_rows), :] + cb_ref[...]
        mu = jnp.mean(y, axis=-1, keepdims=True)
        yc = y - mu
        var = jnp.mean(yc * yc, axis=-1, keepdims=True)
        z = yc * lax.rsqrt(var + EPS) * lg_ref[...] + lb_ref[...]
        o_ref[pl.ds(r0, norm_rows), :] = _silu(z).astype(BF16)
        return carry

    lax.fori_loop(0, tt // norm_rows, norm_chunk, 0)


def conformer_conv(y, conv_w, conv_b, ln_g, ln_b, batch, seq, a_col0, tt=256, rows=16, norm_rows=64):
    m = y.shape[0]
    taps, ch = conv_w.shape
    assert taps - 1 <= CONV_HALO
    nt = seq // tt
    ablk = a_col0 // ch
    per = tt // CONV_HALO

    def prev_map(cb):
        return lambda b, i: (jnp.maximum((b * nt + i) * per - 1, 0), cb)

    kernel = functools.partial(_conv_kernel, tt=tt, taps=taps, rows=rows, norm_rows=norm_rows)
    row = lambda a: a.reshape(1, ch)
    return pl.pallas_call(
        kernel,
        grid=(batch, nt),
        in_specs=[
            pl.BlockSpec((tt, ch), lambda b, i: (b * nt + i, ablk)),
            pl.BlockSpec((tt, ch), lambda b, i: (b * nt + i, ablk + 1)),
            pl.BlockSpec((CONV_HALO, ch), prev_map(ablk)),
            pl.BlockSpec((CONV_HALO, ch), prev_map(ablk + 1)),
            pl.BlockSpec((taps * SUBLANES, ch), lambda b, i: (0, 0)),
            pl.BlockSpec((1, ch), lambda b, i: (0, 0)),
            pl.BlockSpec((1, ch), lambda b, i: (0, 0)),
            pl.BlockSpec((1, ch), lambda b, i: (0, 0)),
        ],
        out_specs=pl.BlockSpec((tt, ch), lambda b, i: (b * nt + i, 0)),
        out_shape=jax.ShapeDtypeStruct((m, ch), BF16),
        scratch_shapes=[pltpu.VMEM((SUBLANES, tt + CONV_HALO, ch), F32), pltpu.VMEM((tt, ch), F32)],
        compiler_params=_cparams("parallel", "arbitrary"),
        name="conformer_conv",
    )(y, y, y, y, jnp.repeat(conv_w, SUBLANES, axis=0), row(conv_b), row(ln_g), row(ln_b))


def _outproj_kernel(a_ref, b_ref, wa_ref, wb_ref, x_ref, g_ref, o_ref, *, a_residue_major):
    a = a_ref[...]
    if a_residue_major:
        a = _dot(_residue_perm_matrix(a.shape[0], True), a).astype(BF16)
    y = _dot(a, wa_ref[...]) + _dot(b_ref[...], wb_ref[...])
    o_ref[...] = x_ref[...] + g_ref[...] * y


def out_proj(a, b, w_stack, layer, x2, ada3, g_idx, seq, a_residue_major=False, tm=512):
    m, d = x2.shape
    ka = a.shape[1]
    kb = b.shape[1]
    assert ka == kb
    assert not a_residue_major or tm == PERM_TILE
    tiles_per_batch = seq // tm
    return pl.pallas_call(
        functools.partial(_outproj_kernel, a_residue_major=a_residue_major),
        grid=(m // tm,),
        in_specs=[
            pl.BlockSpec((tm, ka), lambda i: (i, 0)),
            pl.BlockSpec((tm, kb), lambda i: (i, 0)),
            pl.BlockSpec((None, ka, d), lambda i: (layer, 0, 0)),
            pl.BlockSpec((None, kb, d), lambda i: (layer, 1, 0)),
            pl.BlockSpec((tm, d), lambda i: (i, 0)),
            pl.BlockSpec((None, 1, d), lambda i: (i // tiles_per_batch, 0, g_idx)),
        ],
        out_specs=pl.BlockSpec((tm, d), lambda i: (i, 0)),
        out_shape=jax.ShapeDtypeStruct((m, d), F32),
        compiler_params=_cparams("parallel"),
        name="out_proj",
    )(a, b, w_stack, w_stack, x2, ada3)


def _ffn_kernel(x_ref, g_ref, sc_ref, sh_ref, gate_ref, wg_ref, wu_ref, wd_ref, o_ref, h_scr, acc_scr):
    f = pl.program_id(1)

    @pl.when(f == 0)
    def _():
        h_scr[...] = _norm_mod(x_ref[...], g_ref[...], sc_ref[...], sh_ref[...]).astype(BF16)
        acc_scr[...] = jnp.zeros(acc_scr.shape, F32)

    h = h_scr[...]
    a = _silu(_dot(h, wg_ref[...])) * _dot(h, wu_ref[...])
    acc_scr[...] += _dot(a.astype(BF16), wd_ref[...])

    @pl.when(f == pl.num_programs(1) - 1)
    def _():
        o_ref[...] = x_ref[...] + gate_ref[...] * acc_scr[...]


def ffn_dense(x2, norm_g, ada3, sc_idx, sh_idx, g_idx, wg, wu, wd, layer, seq, tm=512, tf=512):
    m, d = x2.shape
    dff = wg.shape[2]
    tiles_per_batch = seq // tm
    mod = lambda k: pl.BlockSpec((None, 1, d), lambda i, f: (i // tiles_per_batch, 0, k))
    return pl.pallas_call(
        _ffn_kernel,
        grid=(m // tm, dff // tf),
        in_specs=[
            pl.BlockSpec((tm, d), lambda i, f: (i, 0)),
            pl.BlockSpec((1, d), lambda i, f: (0, 0)),
            mod(sc_idx), mod(sh_idx), mod(g_idx),
            pl.BlockSpec((None, d, tf), lambda i, f: (layer, 0, f)),
            pl.BlockSpec((None, d, tf), lambda i, f: (layer, 0, f)),
            pl.BlockSpec((None, tf, d), lambda i, f: (layer, f, 0)),
        ],
        out_specs=pl.BlockSpec((tm, d), lambda i, f: (i, 0)),
        out_shape=jax.ShapeDtypeStruct((m, d), F32),
        scratch_shapes=[pltpu.VMEM((tm, d), BF16), pltpu.VMEM((tm, d), F32)],
        compiler_params=_cparams("parallel", "arbitrary"),
        name="ffn_dense",
    )(x2, norm_g.reshape(1, d), ada3, ada3, ada3, wg, wu, wd)


def _router_kernel(x_ref, g_ref, sc_ref, sh_ref, rw_ref, cols_ref, rows_ref, counts_ref, *, n_experts):
    i = pl.program_id(0)

    @pl.when(i == 0)
    def _():
        counts_ref[...] = jnp.zeros(counts_ref.shape, F32)

    h = _norm_mod(x_ref[...], g_ref[...], sc_ref[...], sh_ref[...])
    rw = rw_ref[...]
    h_hi, rw_hi = h.astype(BF16), rw.astype(BF16)
    h_lo, rw_lo = (h - h_hi.astype(F32)).astype(BF16), (rw - rw_hi.astype(F32)).astype(BF16)
    logits = _dot(h_hi, rw_hi) + _dot(h_hi, rw_lo) + _dot(h_lo, rw_hi)
    tm = logits.shape[0]
    lane = lax.broadcasted_iota(I32, logits.shape, 1)
    lg = jnp.where(lane < n_experts, logits, NEG_INF)
    m1 = jnp.max(lg, axis=-1, keepdims=True)
    i1 = jnp.min(jnp.where(lg == m1, lane, LANES), axis=-1, keepdims=True)
    sel1 = lane == i1
    lg2 = jnp.where(sel1, NEG_INF, lg)
    m2 = jnp.max(lg2, axis=-1, keepdims=True)
    i2 = jnp.min(jnp.where(lg2 == m2, lane, LANES), axis=-1, keepdims=True)
    sel2 = lane == i2
    e = jnp.exp(m2 - m1)
    w1 = 1.0 / (1.0 + e)
    w2 = e / (1.0 + e)

    onehot = jnp.where(jnp.logical_or(sel1, sel2), 1.0, 0.0)
    tri = jnp.where(lax.broadcasted_iota(I32, (tm, tm), 0) > lax.broadcasted_iota(I32, (tm, tm), 1), 1.0, 0.0)
    rank = counts_ref[0:1, :] + _dot(tri.astype(BF16), onehot.astype(BF16))
    rank1 = jnp.sum(jnp.where(sel1, rank, 0.0), axis=-1, keepdims=True)
    rank2 = jnp.sum(jnp.where(sel2, rank, 0.0), axis=-1, keepdims=True)
    counts_ref[0:1, :] = counts_ref[0:1, :] + jnp.sum(onehot, axis=0, keepdims=True)

    cols = jnp.zeros(logits.shape, F32)
    for k, val in enumerate((rank1, rank2, i1.astype(F32), i2.astype(F32), w1, w2)):
        cols = jnp.where(lane == k, val, cols)
    cols_ref[...] = cols
    rows_ref[...] = jnp.transpose(cols)[0:SUBLANES, :].astype(I32)


def moe_router(x2, norm_g, ada3, sc_idx, sh_idx, router_w, seq, tm=512):
    m, d = x2.shape
    n_experts = router_w.shape[1]
    rw = jnp.zeros((d, LANES), F32).at[:, :n_experts].set(router_w)
    tiles_per_batch = seq // tm
    mod = lambda k: pl.BlockSpec((None, 1, d), lambda i: (i // tiles_per_batch, 0, k))
    return pl.pallas_call(
        functools.partial(_router_kernel, n_experts=n_experts),
        grid=(m // tm,),
        in_specs=[
            pl.BlockSpec((tm, d), lambda i: (i, 0)),
            pl.BlockSpec((1, d), lambda i: (0, 0)),
            mod(sc_idx), mod(sh_idx),
            pl.BlockSpec((d, LANES), lambda i: (0, 0)),
        ],
        out_specs=[
            pl.BlockSpec((tm, LANES), lambda i: (i, 0)),
            pl.BlockSpec((SUBLANES, tm), lambda i: (i, 0)),
            pl.BlockSpec((SUBLANES, LANES), lambda i: (0, 0)),
        ],
        out_shape=[
            jax.ShapeDtypeStruct((m, LANES), F32),
            jax.ShapeDtypeStruct((m // tm * SUBLANES, tm), I32),
            jax.ShapeDtypeStruct((SUBLANES, LANES), F32),
        ],
        compiler_params=_cparams("arbitrary"),
        name="moe_router",
    )(x2, norm_g.reshape(1, d), ada3, ada3, rw)


def _row_copy(src, src_row, dst, dst_row, sem):
    return pltpu.make_async_copy(src.at[pl.ds(src_row, 1)], dst.at[pl.ds(dst_row, 1)], sem)


def _dispatch_kernel(x_ref, g_ref, sc_ref, sh_ref, pos_ref, xs_in_ref, xs_ref, hp_scr, pos_smem, sem_pos, sem_rows):
    del xs_in_ref
    tm, half = hp_scr.shape
    pos_copy = pltpu.make_async_copy(pos_ref, pos_smem, sem_pos)
    pos_copy.start()
    h = _norm_mod(x_ref[...], g_ref[...], sc_ref[...], sh_ref[...])
    hp_scr[...] = _pack_bf16_pair(h[:, :half], h[:, half:])
    pos_copy.wait()

    def issue(t, carry):
        _row_copy(hp_scr, t, xs_ref, pos_smem[0, t], sem_rows).start()
        _row_copy(hp_scr, t, xs_ref, pos_smem[1, t], sem_rows).start()
        return carry

    lax.fori_loop(0, tm, issue, 0, unroll=ROW_DMA_UNROLL)

    def drain(t, carry):
        _row_copy(hp_scr, t, xs_ref, pos_smem[0, t], sem_rows).wait()
        _row_copy(hp_scr, t, xs_ref, pos_smem[1, t], sem_rows).wait()
        return carry

    lax.fori_loop(0, tm, drain, 0, unroll=ROW_DMA_UNROLL)


def moe_dispatch(x2, norm_g, ada3, sc_idx, sh_idx, pos_rows, n_rows, seq, tm=512):
    m, d = x2.shape
    half = d // 2
    tiles_per_batch = seq // tm
    mod = lambda k: pl.BlockSpec((None, 1, d), lambda i: (i // tiles_per_batch, 0, k))
    xs0 = jnp.zeros((n_rows, half), U32)
    return pl.pallas_call(
        _dispatch_kernel,
        grid=(m // tm,),
        in_specs=[
            pl.BlockSpec((tm, d), lambda i: (i, 0)),
            pl.BlockSpec((1, d), lambda i: (0, 0)),
            mod(sc_idx), mod(sh_idx),
            pl.BlockSpec((SUBLANES, tm), lambda i: (i, 0)),
            pl.BlockSpec(memory_space=pl.ANY),
        ],
        out_specs=pl.BlockSpec(memory_space=pl.ANY),
        out_shape=jax.ShapeDtypeStruct((n_rows, half), U32),
        scratch_shapes=[
            pltpu.VMEM((tm, half), U32), pltpu.SMEM((SUBLANES, tm), I32),
            pltpu.SemaphoreType.DMA, pltpu.SemaphoreType.DMA,
        ],
        input_output_aliases={5: 0},
        compiler_params=_cparams("arbitrary"),
        name="moe_dispatch",
    )(x2, norm_g.reshape(1, d), ada3, ada3, pos_rows, xs0)


def _moe_ffn_kernel(te_ref, tv_ref, xs_ref, wg_ref, wu_ref, wd_ref, ys_ref, h_scr, acc_scr):
    del te_ref
    g = pl.program_id(0)
    f = pl.program_id(1)
    tm, half = xs_ref.shape
    n_valid = tv_ref[g]
    last = f == pl.num_programs(1) - 1

    def run(n):
        @pl.when(f == 0)
        def _():
            lo, hi = _unpack_bf16_pair(xs_ref[0:n, :])
            h_scr[0:n, :half] = lo.astype(BF16)
            h_scr[0:n, half:] = hi.astype(BF16)
            acc_scr[0:n, :] = jnp.zeros((n, 2 * half), F32)

        h = h_scr[0:n, :]
        a = _silu(_dot(h, wg_ref[...])) * _dot(h, wu_ref[...])
        acc_scr[0:n, :] += _dot(a.astype(BF16), wd_ref[...])

        @pl.when(last)
        def _():
            ys_ref[0:n, :] = _pack_bf16_pair(acc_scr[0:n, :half], acc_scr[0:n, half:])
            if n < tm:
                ys_ref[n:, :] = jnp.zeros((tm - n, half), U32)

    @pl.when(n_valid > tm // 2)
    def _():
        run(tm)

    @pl.when(jnp.logical_and(n_valid > 0, n_valid <= tm // 2))
    def _():
        run(tm // 2)

    @pl.when(jnp.logical_and(n_valid == 0, last))
    def _():
        ys_ref[...] = jnp.zeros(ys_ref.shape, U32)


def moe_expert_ffn(xs, tile_expert, tile_rows, wg, wu, wd, layer, tm, tf=512):
    n_rows, half = xs.shape
    d = 2 * half
    dff = wg.shape[3]
    fidx = lambda f, tv, g: f * jnp.minimum(tv[g], 1)
    grid_spec = pltpu.PrefetchScalarGridSpec(
        num_scalar_prefetch=2,
        grid=(n_rows // tm, dff // tf),
        in_specs=[
            pl.BlockSpec((tm, half), lambda g, f, te, tv: (g, 0)),
            pl.BlockSpec((None, None, d, tf), lambda g, f, te, tv: (layer, te[g], 0, fidx(f, tv, g))),
            pl.BlockSpec((None, None, d, tf), lambda g, f, te, tv: (layer, te[g], 0, fidx(f, tv, g))),
            pl.BlockSpec((None, None, tf, d), lambda g, f, te, tv: (layer, te[g], fidx(f, tv, g), 0)),
        ],
        out_specs=pl.BlockSpec((tm, half), lambda g, f, te, tv: (g, 0)),
        scratch_shapes=[pltpu.VMEM((tm, d), BF16), pltpu.VMEM((tm, d), F32)],
    )
    return pl.pallas_call(
        _moe_ffn_kernel,
        grid_spec=grid_spec,
        out_shape=jax.ShapeDtypeStruct((n_rows, half), U32),
        compiler_params=_cparams("parallel", "arbitrary"),
        name="moe_expert_ffn",
    )(tile_expert, tile_rows, xs, wg, wu, wd)


def _combine_kernel(x_ref, gate_ref, cols_ref, pos_ref, ys_ref, o_ref, r1_scr, r2_scr, pos_smem, sem_pos, sem_rows):
    tm, half = r1_scr.shape
    pos_copy = pltpu.make_async_copy(pos_ref, pos_smem, sem_pos)
    pos_copy.start()
    pos_copy.wait()

    def issue(t, carry):
        _row_copy(ys_ref, pos_smem[0, t], r1_scr, t, sem_rows).start()
        _row_copy(ys_ref, pos_smem[1, t], r2_scr, t, sem_rows).start()
        return carry

    lax.fori_loop(0, tm, issue, 0, unroll=ROW_DMA_UNROLL)

    def drain(t, carry):
        _row_copy(ys_ref, pos_smem[0, t], r1_scr, t, sem_rows).wait()
        _row_copy(ys_ref, pos_smem[1, t], r2_scr, t, sem_rows).wait()
        return carry

    lax.fori_loop(0, tm, drain, 0, unroll=ROW_DMA_UNROLL)

    w1 = cols_ref[:, 4:5]
    w2 = cols_ref[:, 5:6]
    lo1, hi1 = _unpack_bf16_pair(r1_scr[...])
    lo2, hi2 = _unpack_bf16_pair(r2_scr[...])
    o_ref[:, :half] = x_ref[:, :half] + gate_ref[:, :half] * (w1 * lo1 + w2 * lo2)
    o_ref[:, half:] = x_ref[:, half:] + gate_ref[:, half:] * (w1 * hi1 + w2 * hi2)


def moe_combine(x2, ada3, g_idx, cols, pos_rows, ys, seq, tm=512):
    m, d = x2.shape
    half = d // 2
    tiles_per_batch = seq // tm
    return pl.pallas_call(
        _combine_kernel,
        grid=(m // tm,),
        in_specs=[
            pl.BlockSpec((tm, d), lambda i: (i, 0)),
            pl.BlockSpec((None, 1, d), lambda i: (i // tiles_per_batch, 0, g_idx)),
            pl.BlockSpec((tm, LANES), lambda i: (i, 0)),
            pl.BlockSpec((SUBLANES, tm), lambda i: (i, 0)),
            pl.BlockSpec(memory_space=pl.ANY),
        ],
        out_specs=pl.BlockSpec((tm, d), lambda i: (i, 0)),
        out_shape=jax.ShapeDtypeStruct((m, d), F32),
        scratch_shapes=[
            pltpu.VMEM((tm, half), U32), pltpu.VMEM((tm, half), U32), pltpu.SMEM((SUBLANES, tm), I32),
            pltpu.SemaphoreType.DMA, pltpu.SemaphoreType.DMA,
        ],
        compiler_params=_cparams("arbitrary"),
        name="moe_combine",
    )(x2, ada3, cols, pos_rows, ys)


def moe_ffn(x2, norm_g, ada3, sc_idx, sh_idx, g_idx, router_w, wg, wu, wd, layer, seq, tm=512,
            row_tile=MOE_ROW_TILE):
    m = x2.shape[0]
    n_experts = router_w.shape[1]
    row_tile = min(row_tile, m)
    cols, rows, counts = moe_router(x2, norm_g, ada3, sc_idx, sh_idx, router_w, seq, tm=tm)
    n_rows = 2 * m + n_experts * row_tile
    counts = counts[0, :n_experts].astype(I32)
    padded = ((counts + row_tile - 1) // row_tile) * row_tile
    ends = jnp.cumsum(padded)
    offs = ends - padded
    rows3 = rows.reshape(m // tm, SUBLANES, tm)
    pos_rows = rows3.at[:, 0].add(offs[rows3[:, 2]]).at[:, 1].add(offs[rows3[:, 3]]).reshape(rows.shape)
    tile_start = jnp.arange(n_rows // row_tile, dtype=I32) * row_tile
    tile_expert = jnp.minimum(jnp.sum(tile_start[:, None] >= ends[None, :], axis=1), n_experts - 1).astype(I32)
    tile_rows = jnp.clip((offs + counts)[tile_expert] - tile_start, 0, row_tile).astype(I32)

    xs = moe_dispatch(x2, norm_g, ada3, sc_idx, sh_idx, pos_rows, n_rows, seq, tm=tm)
    ys = moe_expert_ffn(xs, tile_expert, tile_rows, wg, wu, wd, layer, row_tile)
    return moe_combine(x2, ada3, g_idx, cols, pos_rows, ys, seq, tm=tm)


def _head_gain(q_g, k_g, n_q_heads, n_k_heads, q_scale):
    return jnp.concatenate([jnp.tile(q_g * q_scale, n_q_heads), jnp.tile(k_g, n_k_heads)]).reshape(1, -1).astype(F32)


def kernel(x, c, ada_w, ada_b, norm_mix_g, norm_ffn_g, ev_w_in, ev_w_out, a_q_norm_g, a_k_norm_g, a_lambda,
           a_head_norm_g, ffn_w_gate, ffn_w_up, ffn_w_down, od_w_in, od_w_out, c_q_norm_g, c_k_norm_g,
           d_conv_w, d_conv_b, d_ln_g, d_ln_b, moe_router_w, moe_w_gate, moe_w_up, moe_w_down):
    batch, seq, d = x.shape
    depth = ada_w.shape[0]
    m = batch * seq
    scale = HEAD_DIM ** -0.5
    a_width = ev_w_out.shape[1] // 2
    a_heads = a_width // (2 * HEAD_DIM)
    b_heads = a_width // HEAD_DIM
    c_width = od_w_out.shape[1] // 2
    c_heads = c_width // HEAD_DIM

    ev_w_in, ev_w_out, od_w_in, od_w_out = (w.astype(BF16) for w in (ev_w_in, ev_w_out, od_w_in, od_w_out))
    ffn_w = tuple(w.astype(BF16) for w in (ffn_w_gate, ffn_w_up, ffn_w_down))
    moe_w = tuple(w.astype(BF16) for w in (moe_w_gate, moe_w_up, moe_w_down))

    ada = ada_all(c, ada_w, ada_b)
    x2 = x.reshape(m, d)
    for i in range(depth):
        j = i // 2
        ada3 = ada[i].reshape(batch, 1, 6 * d)
        if i % 2 == 0:
            lam_init = 0.8 - 0.6 * math.exp(-0.3 * i)
            lv = a_lambda[j].astype(F32)
            lam = jnp.exp(jnp.sum(lv[0] * lv[1])) - jnp.exp(jnp.sum(lv[2] * lv[3])) + lam_init
            slopes = 2.0 ** (-8.0 * jnp.arange(1, a_heads + 1, dtype=F32) / a_heads)
            gain = _head_gain(a_q_norm_g[j], a_k_norm_g[j], 2 * a_heads, 2 * a_heads, scale * LOG2E)
            (y,) = in_proj(x2, norm_mix_g[i], ada3, 1, 0, ev_w_in, j, gain, seq)
            oa = diff_attention(y, slopes, lam.reshape(1), a_head_norm_g[j], batch, seq, a_heads,
                                k_col0=a_width, v_col0=2 * a_width, out_scale=1.0 - lam_init)
            ob = stick_attention(y, batch, seq, b_heads, q_col0=3 * a_width, k_col0=4 * a_width,
                                 v_col0=5 * a_width)
            x2 = out_proj(oa, ob, ev_w_out, j, x2, ada3, 2, seq)
            x2 = ffn_dense(x2, norm_ffn_g[i], ada3, 4, 3, 5, *ffn_w, j, seq)
        else:
            gain = _head_gain(c_q_norm_g[j], c_k_norm_g[j], c_heads, c_heads, scale)
            qkv, du = in_proj(x2, norm_mix_g[i], ada3, 1, 0, od_w_in, j, gain, seq, n_first=3 * c_width,
                              perm_first=True)
            oc = dilated_mixture(qkv, batch, seq, c_heads)
            od = conformer_conv(du, d_conv_w[j], d_conv_b[j], d_ln_g[j], d_ln_b[j], batch, seq, a_col0=0)
            x2 = out_proj(oc, od, od_w_out, j, x2, ada3, 2, seq, a_residue_major=True)
            x2 = moe_ffn(x2, norm_ffn_g[i], ada3, 4, 3, 5, moe_router_w[j], *moe_w, j, seq)
    return x2.reshape(batch, seq, d)
```

```python
import functools
import math

import jax
import jax.numpy as jnp
from jax import lax
from jax.experimental import pallas as pl
from jax.experimental.pallas import tpu as pltpu

F32 = jnp.float32
BF16 = jnp.bfloat16
U32 = jnp.uint32
I32 = jnp.int32
EPS = 1e-6
HEAD_DIM = 128
LANES = 128
SUBLANES = 8
VMEM_LIMIT_BYTES = 56 * 1024 * 1024
CONV_HALO = 32
STICK_UNDERFLOW = -104.0
DILATED = ((128, 1), (512, 4), (2048, 16))
NEG_INF = float("-inf")
LOG2E = math.log2(math.e)
MOE_ROW_TILE = 1024
ROW_DMA_UNROLL = 8
RESIDUES = 16
PERM_TILE = 512
BAND_STEPS = 128


def _cparams(*sem):
    return pltpu.CompilerParams(dimension_semantics=sem, vmem_limit_bytes=VMEM_LIMIT_BYTES)


def _dot(a, b):
    return jnp.dot(a, b, preferred_element_type=F32)


def _dot_nt(a, b):
    return lax.dot_general(a, b, (((1,), (1,)), ((), ())), preferred_element_type=F32)


def _silu(x):
    return x * (1.0 / (1.0 + jnp.exp(-x)))


def _norm_mod(x, g, sc, sh):
    ms = jnp.mean(x * x, axis=-1, keepdims=True)
    return (x * lax.rsqrt(ms + EPS) * g) * (1.0 + sc) + sh


def _pack_bf16_pair(lo, hi):
    lo_bits = lax.bitcast_convert_type(lo.astype(BF16).astype(F32), U32)
    hi_bits = lax.bitcast_convert_type(hi.astype(BF16).astype(F32), U32)
    return hi_bits | (lo_bits >> 16)


def _unpack_bf16_pair(w):
    lo = lax.bitcast_convert_type(w << 16, F32)
    hi = lax.bitcast_convert_type(w & U32(0xFFFF0000), F32)
    return lo, hi


def _ada_kernel(c_ref, w_ref, b_ref, o_ref):
    cond = _silu(c_ref[...])
    o_ref[...] = jnp.dot(cond, w_ref[...], preferred_element_type=F32,
                         precision=lax.Precision.HIGHEST) + b_ref[...]


def ada_all(c, ada_w, ada_b, tn=1024):
    depth, d, n = ada_w.shape
    b = c.shape[0]
    return pl.pallas_call(
        _ada_kernel,
        grid=(depth, n // tn),
        in_specs=[
            pl.BlockSpec((b, d), lambda i, j: (0, 0)),
            pl.BlockSpec((None, d, tn), lambda i, j: (i, 0, j)),
            pl.BlockSpec((None, 1, tn), lambda i, j: (i, 0, j)),
        ],
        out_specs=pl.BlockSpec((None, b, tn), lambda i, j: (i, 0, j)),
        out_shape=jax.ShapeDtypeStruct((depth, b, n), F32),
        compiler_params=_cparams("parallel", "parallel"),
        name="ada",
    )(c, ada_w, ada_b.reshape(depth, 1, n))


def _residue_perm_matrix(n, inverse):
    c = n // RESIDUES
    i = lax.broadcasted_iota(I32, (n, n), 1 if inverse else 0)
    t = lax.broadcasted_iota(I32, (n, n), 0 if inverse else 1)
    return jnp.where(t == _interleave_pos(i, c, RESIDUES), 1.0, 0.0).astype(BF16)


def _interleave_pos(i, inner, outer):
    shift = inner.bit_length() - 1
    assert inner == 1 << shift
    return (i & (inner - 1)) * outer + (i >> shift)


def _inproj_kernel(x_ref, g_ref, sc_ref, sh_ref, w_ref, hg_ref, *rest, n_out, n_norm_tiles, n_first_tiles,
                   perm_first):
    o_refs, scr = rest[:n_out], rest[n_out:]
    h_scr = scr[0]
    first_lhs = scr[1] if perm_first else h_scr
    j = pl.program_id(1)

    @pl.when(j == 0)
    def _():
        h = _norm_mod(x_ref[...], g_ref[...], sc_ref[...], sh_ref[...]).astype(BF16)
        h_scr[...] = h
        if perm_first:
            first_lhs[...] = _dot(_residue_perm_matrix(h.shape[0], False), h).astype(BF16)

    @pl.when(j < n_first_tiles)
    def _():
        y = _dot(first_lhs[...], w_ref[...])
        tn = y.shape[1]

        @pl.when(j < n_norm_tiles)
        def _():
            for c in range(tn // HEAD_DIM):
                sl = slice(c * HEAD_DIM, (c + 1) * HEAD_DIM)
                blk = y[:, sl]
                r = lax.rsqrt(jnp.mean(blk * blk, axis=-1, keepdims=True) + EPS)
                o_refs[0][:, sl] = (blk * r * hg_ref[:, sl]).astype(BF16)

        @pl.when(j >= n_norm_tiles)
        def _():
            o_refs[0][...] = y.astype(BF16)

    if n_out > 1:
        @pl.when(j >= n_first_tiles)
        def _():
            o_refs[1][...] = _dot(h_scr[...], w_ref[...]).astype(BF16)


def in_proj(x2, norm_g, ada3, sc_idx, sh_idx, w_stack, layer, head_gain, seq, n_first=None, perm_first=False,
            tm=512, tn=1024):
    m, d = x2.shape
    n = w_stack.shape[2]
    n_first = n if n_first is None else n_first
    n_norm_tiles = head_gain.shape[1] // tn
    n_first_tiles = n_first // tn
    assert n_norm_tiles <= n_first_tiles
    assert not perm_first or tm == PERM_TILE
    tiles_per_batch = seq // tm
    out_specs = [pl.BlockSpec((tm, tn), lambda i, j: (i, jnp.minimum(j, n_first_tiles - 1)))]
    out_shape = [jax.ShapeDtypeStruct((m, n_first), BF16)]
    if n_first < n:
        out_specs.append(pl.BlockSpec((tm, tn), lambda i, j: (i, jnp.maximum(j - n_first_tiles, 0))))
        out_shape.append(jax.ShapeDtypeStruct((m, n - n_first), BF16))
    return pl.pallas_call(
        functools.partial(_inproj_kernel, n_out=len(out_specs), n_norm_tiles=n_norm_tiles,
                          n_first_tiles=n_first_tiles, perm_first=perm_first),
        grid=(m // tm, n // tn),
        in_specs=[
            pl.BlockSpec((tm, d), lambda i, j: (i, 0)),
            pl.BlockSpec((1, d), lambda i, j: (0, 0)),
            pl.BlockSpec((None, 1, d), lambda i, j: (i // tiles_per_batch, 0, sc_idx)),
            pl.BlockSpec((None, 1, d), lambda i, j: (i // tiles_per_batch, 0, sh_idx)),
            pl.BlockSpec((None, d, tn), lambda i, j: (layer, 0, j)),
            pl.BlockSpec((1, tn), lambda i, j: (0, jnp.minimum(j, n_norm_tiles - 1))),
        ],
        out_specs=out_specs,
        out_shape=out_shape,
        scratch_shapes=[pltpu.VMEM((tm, d), BF16)] * (2 if perm_first else 1),
        compiler_params=_cparams("parallel", "arbitrary"),
        name="in_proj",
    )(x2, norm_g.reshape(1, d), ada3, ada3, w_stack, head_gain)


def _diff_attn_kernel(slopes_ref, lam_ref, q_ref, k_ref, v_ref, hg_ref, o_ref, m_scr, l_scr, a_scr,
                      *, tq, out_scale, heads):
    hg0 = pl.program_id(1) * heads
    qi = pl.program_id(2)
    hw = 2 * HEAD_DIM
    lam = lam_ref[0]
    col = lax.broadcasted_iota(I32, (1, tq), 1).astype(F32)
    reps = tq // LANES
    maps = [(h, a) for h in range(heads) for a in range(2)]
    qsl = [slice(h * hw + a * HEAD_DIM, h * hw + (a + 1) * HEAD_DIM) for h, a in maps]
    vsl = [slice(h * hw, (h + 1) * hw) for h, _ in maps]

    m_scr[...] = jnp.full(m_scr.shape, NEG_INF, F32)
    l_scr[...] = jnp.zeros(l_scr.shape, F32)
    a_scr[...] = jnp.zeros(a_scr.shape, F32)

    def block(kb, masked):
        ks = pl.multiple_of(kb * tq, tq)
        kpos = col + ((kb - qi) * tq).astype(F32)
        cbias = [slopes_ref[hg0 + h] * LOG2E * kpos for h in range(heads)]
        ss = [_dot_nt(q_ref[:, sl], k_ref[pl.ds(ks, tq), sl]) + cbias[h] for sl, (h, _) in zip(qsl, maps)]
        if masked:
            rc = lax.broadcasted_iota(I32, (tq, tq), 0) - lax.broadcasted_iota(I32, (tq, tq), 1)
            ss = [jnp.where(rc >= 0, s, NEG_INF) for s in ss]
        m_olds = [m_scr[i] for i in range(len(maps))]
        m_news = [jnp.maximum(m_old, jnp.max(s, axis=-1, keepdims=True)) for m_old, s in zip(m_olds, ss)]
        ps = [jnp.exp2(s - jnp.tile(m_new, (1, reps))) for s, m_new in zip(ss, m_news)]
        pvs = [_dot(p.astype(BF16), v_ref[pl.ds(ks, tq), sl]) for p, sl in zip(ps, vsl)]
        for i in range(len(maps)):
            alpha = jnp.exp2(m_olds[i] - m_news[i])
            l_scr[i] = alpha * l_scr[i] + jnp.sum(ps[i], axis=-1, keepdims=True)
            a_scr[i] = jnp.tile(alpha, (1, 2)) * a_scr[i] + pvs[i]
            m_scr[i] = m_news[i]

    def body(kb, carry):
        block(kb, False)
        return carry

    lax.fori_loop(0, qi, body, 0)
    block(qi, True)

    for h in range(heads):
        inv1 = jnp.tile(1.0 / l_scr[2 * h], (1, 2))
        inv2 = jnp.tile(1.0 / l_scr[2 * h + 1], (1, 2))
        o = a_scr[2 * h] * inv1 - lam * (a_scr[2 * h + 1] * inv2)
        r = lax.rsqrt(jnp.mean(o * o, axis=-1, keepdims=True) + EPS)
        o_ref[:, h * hw:(h + 1) * hw] = (o * r * hg_ref[...] * out_scale).astype(BF16)


def diff_attention(y, slopes, lam, head_g, batch, seq, n_heads, k_col0, v_col0, out_scale, tq=512, heads=2):
    m = y.shape[0]
    hw = 2 * HEAD_DIM
    bw = heads * hw
    nq = seq // tq
    kernel = functools.partial(_diff_attn_kernel, tq=tq, out_scale=out_scale, heads=heads)
    return pl.pallas_call(
        kernel,
        grid=(batch, n_heads // heads, nq),
        in_specs=[
            pl.BlockSpec(memory_space=pltpu.SMEM),
            pl.BlockSpec(memory_space=pltpu.SMEM),
            pl.BlockSpec((tq, bw), lambda b, h, qi: (b * nq + qi, h)),
            pl.BlockSpec((seq, bw), lambda b, h, qi: (b, k_col0 // bw + h)),
            pl.BlockSpec((seq, bw), lambda b, h, qi: (b, v_col0 // bw + h)),
            pl.BlockSpec((1, hw), lambda b, h, qi: (0, 0)),
        ],
        out_specs=pl.BlockSpec((tq, bw), lambda b, h, qi: (b * nq + qi, h)),
        out_shape=jax.ShapeDtypeStruct((m, n_heads * hw), BF16),
        scratch_shapes=[
            pltpu.VMEM((2 * heads, tq, LANES), F32), pltpu.VMEM((2 * heads, tq, LANES), F32),
            pltpu.VMEM((2 * heads, tq, hw), F32),
        ],
        compiler_params=_cparams("parallel", "parallel", "arbitrary"),
        name="diff_attn",
    )(slopes, lam, y, y, y, head_g.reshape(1, hw))


def _stick_kernel(q_ref, k_ref, v_ref, o_ref, *scratch, tq, tk, scale, heads):
    acc_scrs, carry_scrs = scratch[:heads], scratch[heads:]
    qi = pl.program_id(2)
    rc = lax.broadcasted_iota(I32, (tq, tk), 0) - lax.broadcasted_iota(I32, (tq, tk), 1)
    jj = lax.broadcasted_iota(I32, (tk, 2 * tk), 0)
    ss = lax.broadcasted_iota(I32, (tk, 2 * tk), 1)
    suffix = jnp.where(jnp.logical_or(jj > ss, ss >= tk), 1.0, 0.0).astype(BF16)

    for ref in scratch:
        ref[...] = jnp.zeros(ref.shape, F32)
    kb0 = (qi * tq + tq) // tk - 1

    def cond(state):
        kb, live = state
        return jnp.logical_and(kb >= 0, live > STICK_UNDERFLOW)

    def body(state):
        kb, _ = state
        ks = pl.multiple_of(kb * tk, tk)
        past = rc + (qi * tq - kb * tk) > 0
        sls = [slice(h * HEAD_DIM, (h + 1) * HEAD_DIM) for h in range(heads)]
        zs = [_dot_nt((q_ref[:, sl].astype(F32) * scale).astype(BF16), k_ref[pl.ds(ks, tk), sl]) for sl in sls]
        lsigs = [jnp.minimum(z, 0.0) - jnp.log(1.0 + jnp.exp(-jnp.abs(z))) for z in zs]
        l1ms = [jnp.where(past, lsig - z, 0.0) for lsig, z in zip(lsigs, zs)]
        his = [l1m.astype(BF16) for l1m in l1ms]
        los = [(l1m - hi.astype(F32)).astype(BF16) for l1m, hi in zip(l1ms, his)]
        sums = [_dot(hi, suffix) + _dot(lo, suffix) for hi, lo in zip(his, los)]
        carries = [ref[...] for ref in carry_scrs]
        ws = [jnp.where(past, jnp.exp(lsig + carry + sm[:, :tk]), 0.0).astype(BF16)
              for lsig, carry, sm in zip(lsigs, carries, sums)]
        pvs = [_dot(w, v_ref[pl.ds(ks, tk), sl]) for w, sl in zip(ws, sls)]
        live = jnp.full((tq, tk), NEG_INF, F32)
        for h in range(heads):
            acc_scrs[h][...] += pvs[h]
            carry = carries[h] + sums[h][:, tk:]
            carry_scrs[h][...] = carry
            live = jnp.maximum(live, carry)
        return kb - 1, jnp.max(live)

    lax.while_loop(cond, body, (kb0, jnp.float32(0.0)))
    for h in range(heads):
        o_ref[:, h * HEAD_DIM:(h + 1) * HEAD_DIM] = acc_scrs[h][...].astype(BF16)


def stick_attention(y, batch, seq, n_heads, q_col0, k_col0, v_col0, tq=256, tk=128, heads=4):
    m = y.shape[0]
    nq = seq // tq
    bw = heads * HEAD_DIM
    assert tk == LANES
    kernel = functools.partial(_stick_kernel, tq=tq, tk=tk, scale=HEAD_DIM ** -0.5, heads=heads)
    return pl.pallas_call(
        kernel,
        grid=(batch, n_heads // heads, nq),
        in_specs=[
            pl.BlockSpec((tq, bw), lambda b, h, qi: (b * nq + qi, q_col0 // bw + h)),
            pl.BlockSpec((seq, bw), lambda b, h, qi: (b, k_col0 // bw + h)),
            pl.BlockSpec((seq, bw), lambda b, h, qi: (b, v_col0 // bw + h)),
        ],
        out_specs=pl.BlockSpec((tq, bw), lambda b, h, qi: (b * nq + qi, h)),
        out_shape=jax.ShapeDtypeStruct((m, n_heads * HEAD_DIM), BF16),
        scratch_shapes=[pltpu.VMEM((tq, HEAD_DIM), F32)] * heads + [pltpu.VMEM((tq, tk), F32)] * heads,
        compiler_params=_cparams("parallel", "parallel", "arbitrary"),
        name="stick_attn",
    )(y, y, y)


def _dilated_kernel(*refs, n_heads, dilation, has_state, final, qb, pos_inner, pos_outer):
    if has_state:
        q_ref, kc_ref, kp_ref, vc_ref, vp_ref, acc_in_ref, m_in_ref, l_in_ref = refs[:8]
        outs = refs[8:]
    else:
        q_ref, kc_ref, kp_ref, vc_ref, vp_ref = refs[:5]
        outs = refs[5:]
    nb = pl.program_id(2)
    blk_shape = q_ref.shape

    def rows2d(ref):
        return ref[...].reshape(qb, ref.shape[-1])

    q, kc, kp, vc, vp = (rows2d(r) for r in (q_ref, kc_ref, kp_ref, vc_ref, vp_ref))
    row = lax.broadcasted_iota(I32, (qb, qb), 0)
    col = lax.broadcasted_iota(I32, (qb, qb), 1)
    if pos_outer > 1:
        row, col = _interleave_pos(row, pos_inner, pos_outer), _interleave_pos(col, pos_inner, pos_outer)
    steps_c = row - col
    steps_p = steps_c + qb
    in_band_c = jnp.abs(steps_c - BAND_STEPS // 2) <= BAND_STEPS // 2
    in_band_p = steps_p <= jnp.where(nb > 0, BAND_STEPS, 0)
    dist_c = steps_c.astype(F32)
    dist_p = steps_p.astype(F32)
    reps = qb // HEAD_DIM

    sls = [slice(h * HEAD_DIM, (h + 1) * HEAD_DIM) for h in range(n_heads)]
    slopes = [2.0 ** (-8.0 * (h + 1) / n_heads) * dilation for h in range(n_heads)]
    s_cs = [_dot_nt(q[:, sl], kc[:, sl]) + jnp.where(in_band_c, -slope * dist_c, NEG_INF)
            for sl, slope in zip(sls, slopes)]
    s_ps = [_dot_nt(q[:, sl], kp[:, sl]) + jnp.where(in_band_p, -slope * dist_p, NEG_INF)
            for sl, slope in zip(sls, slopes)]
    m_blks = [jnp.maximum(jnp.max(s_c, axis=-1, keepdims=True), jnp.max(s_p, axis=-1, keepdims=True))
              for s_c, s_p in zip(s_cs, s_ps)]
    if has_state:
        acc_in, m_in, l_in = rows2d(acc_in_ref), rows2d(m_in_ref), rows2d(l_in_ref)
        m_news = [jnp.maximum(m_in[:, sl], m) for m, sl in zip(m_blks, sls)]
    else:
        m_news = [jnp.broadcast_to(m, (qb, HEAD_DIM)) for m in m_blks]
    m_wide = [jnp.tile(m, (1, reps)) if reps > 1 else m for m in m_news]
    p_cs = [jnp.exp(s_c - m) for s_c, m in zip(s_cs, m_wide)]
    p_ps = [jnp.exp(s_p - m) for s_p, m in zip(s_ps, m_wide)]
    accs = [_dot(p_c.astype(BF16), vc[:, sl]) + _dot(p_p.astype(BF16), vp[:, sl])
            for p_c, p_p, sl in zip(p_cs, p_ps, sls)]
    acc_cols, m_cols, l_cols = [], [], []
    for h in range(n_heads):
        l = jnp.sum(p_cs[h], axis=-1, keepdims=True) + jnp.sum(p_ps[h], axis=-1, keepdims=True)
        acc = accs[h]
        if has_state:
            r_in = jnp.exp(m_in[:, sls[h]] - m_news[h])
            l = l + l_in[:, sls[h]] * r_in
            acc = acc + acc_in[:, sls[h]] * r_in
        else:
            l = jnp.broadcast_to(l, (qb, HEAD_DIM))
        if final:
            acc_cols.append((acc / l).astype(BF16))
        else:
            acc_cols.append(acc)
            m_cols.append(m_news[h])
            l_cols.append(l)
    outs[0][...] = jnp.concatenate(acc_cols, axis=1).reshape(blk_shape)
    if not final:
        outs[1][...] = jnp.concatenate(m_cols, axis=1).reshape(blk_shape)
        outs[2][...] = jnp.concatenate(l_cols, axis=1).reshape(blk_shape)


def dilated_branch(qkv, state, batch, seq, n_heads, dilation, final):
    m, n = qkv.shape
    hw = n_heads * HEAD_DIM
    tiles = m // PERM_TILE
    tpb = seq // PERM_TILE
    rows = PERM_TILE // RESIDUES
    if dilation == RESIDUES:
        span = BAND_STEPS // rows
        assert tpb % span == 0
        qb, pos = BAND_STEPS, (1, 1)
        grid = (batch, RESIDUES, tpb // span)
        view = lambda a: a.reshape(tiles, RESIDUES, rows, a.shape[-1])
        blk = lambda w: (span, None, rows, w)
        idx = lambda b, r, i, g: (b * (tpb // span) + i, r, 0, g)
    elif dilation * BAND_STEPS == PERM_TILE:
        sub = RESIDUES // dilation
        qb, pos = BAND_STEPS, (rows, sub)
        grid = (batch, dilation, tpb)
        view = lambda a: a.reshape(tiles, sub, dilation, rows, a.shape[-1])
        blk = lambda w: (None, sub, None, rows, w)
        idx = lambda b, r, i, g: (b * tpb + i, 0, r, 0, g)
    else:
        assert dilation == 1
        qb = 2 * BAND_STEPS
        sub = qb // RESIDUES
        per_tile = rows // sub
        qb, pos = qb, (sub, RESIDUES)
        grid = (batch, 1, tpb * per_tile)
        view = lambda a: a.reshape(tiles, RESIDUES, rows, a.shape[-1])
        blk = lambda w: (None, RESIDUES, sub, w)
        idx = lambda b, r, i, g: (b * tpb + i // per_tile, 0, i % per_tile, g)

    def spec(w, g, back):
        return pl.BlockSpec(blk(w), lambda b, r, i: idx(b, r, jnp.maximum(i - back, 0), g))

    qv = view(qkv)
    in_specs = [spec(hw, 0, 0), spec(hw, 1, 0), spec(hw, 1, 1), spec(hw, 2, 0), spec(hw, 2, 1)]
    args = [qv] * 5
    if state is not None:
        in_specs += [spec(hw, 0, 0)] * 3
        args += [view(a) for a in state]

    def shape(dt):
        return jax.ShapeDtypeStruct(jax.eval_shape(view, jax.ShapeDtypeStruct((m, hw), dt)).shape, dt)

    if final:
        out_specs = [spec(hw, 0, 0)]
        out_shape = [shape(BF16)]
    else:
        out_specs = [spec(hw, 0, 0)] * 3
        out_shape = [shape(F32)] * 3
    kernel = functools.partial(_dilated_kernel, n_heads=n_heads, dilation=dilation, has_state=state is not None,
                               final=final, qb=qb, pos_inner=pos[0], pos_outer=pos[1])
    outs = pl.pallas_call(
        kernel, grid=grid, in_specs=in_specs, out_specs=out_specs, out_shape=out_shape,
        compiler_params=_cparams("parallel", "parallel", "arbitrary"),
        name=f"dilated_d{dilation}",
    )(*args)
    if final:
        return outs[0].reshape(m, hw)
    return tuple(o.reshape(m, hw) for o in outs)


def dilated_mixture(qkv, batch, seq, n_heads):
    state = None
    branches = sorted(DILATED, key=lambda wd: -wd[1])
    for idx, (window, dilation) in enumerate(branches):
        assert window // dilation == BAND_STEPS
        state = dilated_branch(qkv, state, batch, seq, n_heads, dilation, final=idx == len(branches) - 1)
    return state


def _conv_kernel(a_ref, g_ref, ap_ref, gp_ref, w_ref, cb_ref, lg_ref, lb_ref, o_ref, hs_scr, y_scr,
                 *, tt, taps, rows, norm_rows):
    i = pl.program_id(1)
    halo = hs_scr.shape[1] - tt
    n_ch = hs_scr.shape[2]
    off = halo - (taps - 1)

    def glu(a, g):
        return a.astype(F32) * (1.0 / (1.0 + jnp.exp(-g.astype(F32))))

    hs_scr[0, 0:halo, :] = jnp.where(i > 0, glu(ap_ref[...], gp_ref[...]), 0.0)
    hs_scr[0, halo:, :] = glu(a_ref[...], g_ref[...])

    taps_of = {b: [a for a in range(halo // SUBLANES + 1) if 0 <= SUBLANES * a + b - off < taps]
               for b in range(SUBLANES)}
    for b in range(1, SUBLANES):
        if taps_of[b]:
            span = tt + SUBLANES * max(taps_of[b])
            hs_scr[b, 0:span, :] = hs_scr[0, b:b + span, :]

    def conv_chunk(c, carry):
        r0 = pl.multiple_of(c * rows, rows)
        strips = [slice(s * LANES, (s + 1) * LANES) for s in range(n_ch // LANES)]
        accs = [None] * len(strips)
        for b in range(SUBLANES):
            for a in taps_of[b]:
                j = SUBLANES * a + b - off
                for s, ls in enumerate(strips):
                    w_tap = jnp.tile(w_ref[SUBLANES * j:SUBLANES * (j + 1), ls], (rows // SUBLANES, 1))
                    term = hs_scr[b, pl.ds(r0 + SUBLANES * a, rows), ls] * w_tap
                    accs[s] = term if accs[s] is None else accs[s] + term
        for s, ls in enumerate(strips):
            y_scr[pl.ds(r0, rows), ls] = accs[s]
        return carry

    lax.fori_loop(0, tt // rows, conv_chunk, 0)

    def norm_chunk(c, carry):
        r0 = pl.multiple_of(c * norm_rows, norm_rows)
        y = y_scr[pl.ds(r0, norm_rows), :] + cb_ref[...]
        mu = jnp.mean(y, axis=-1, keepdims=True)
        yc = y - mu
        var = jnp.mean(yc * yc, axis=-1, keepdims=True)
        z = yc * lax.rsqrt(var + EPS) * lg_ref[...] + lb_ref[...]
        o_ref[pl.ds(r0, norm_rows), :] = _silu(z).astype(BF16)
        return carry

    lax.fori_loop(0, tt // norm_rows, norm_chunk, 0)


def conformer_conv(y, conv_w, conv_b, ln_g, ln_b, batch, seq, a_col0, tt=256, rows=16, norm_rows=128):
    m = y.shape[0]
    taps, ch = conv_w.shape
    assert taps - 1 <= CONV_HALO
    nt = seq // tt
    ablk = a_col0 // ch
    per = tt // CONV_HALO

    def prev_map(cb):
        return lambda b, i: (jnp.maximum((b * nt + i) * per - 1, 0), cb)

    kernel = functools.partial(_conv_kernel, tt=tt, taps=taps, rows=rows, norm_rows=norm_rows)
    row = lambda a: a.reshape(1, ch)
    return pl.pallas_call(
        kernel,
        grid=(batch, nt),
        in_specs=[
            pl.BlockSpec((tt, ch), lambda b, i: (b * nt + i, ablk)),
            pl.BlockSpec((tt, ch), lambda b, i: (b * nt + i, ablk + 1)),
            pl.BlockSpec((CONV_HALO, ch), prev_map(ablk)),
            pl.BlockSpec((CONV_HALO, ch), prev_map(ablk + 1)),
            pl.BlockSpec((taps * SUBLANES, ch), lambda b, i: (0, 0)),
            pl.BlockSpec((1, ch), lambda b, i: (0, 0)),
            pl.BlockSpec((1, ch), lambda b, i: (0, 0)),
            pl.BlockSpec((1, ch), lambda b, i: (0, 0)),
        ],
        out_specs=pl.BlockSpec((tt, ch), lambda b, i: (b * nt + i, 0)),
        out_shape=jax.ShapeDtypeStruct((m, ch), BF16),
        scratch_shapes=[pltpu.VMEM((SUBLANES, tt + CONV_HALO, ch), F32), pltpu.VMEM((tt, ch), F32)],
        compiler_params=_cparams("parallel", "arbitrary"),
        name="conformer_conv",
    )(y, y, y, y, jnp.repeat(conv_w, SUBLANES, axis=0), row(conv_b), row(ln_g), row(ln_b))


def _outproj_kernel(a_ref, b_ref, wa_ref, wb_ref, x_ref, g_ref, o_ref, *, a_residue_major):
    a = a_ref[...]
    if a_residue_major:
        a = _dot(_residue_perm_matrix(a.shape[0], True), a).astype(BF16)
    y = _dot(a, wa_ref[...]) + _dot(b_ref[...], wb_ref[...])
    o_ref[...] = x_ref[...] + g_ref[...] * y


def out_proj(a, b, w_stack, layer, x2, ada3, g_idx, seq, a_residue_major=False, tm=512):
    m, d = x2.shape
    ka = a.shape[1]
    kb = b.shape[1]
    assert ka == kb
    assert not a_residue_major or tm == PERM_TILE
    tiles_per_batch = seq // tm
    return pl.pallas_call(
        functools.partial(_outproj_kernel, a_residue_major=a_residue_major),
        grid=(m // tm,),
        in_specs=[
            pl.BlockSpec((tm, ka), lambda i: (i, 0)),
            pl.BlockSpec((tm, kb), lambda i: (i, 0)),
            pl.BlockSpec((None, ka, d), lambda i: (layer, 0, 0)),
            pl.BlockSpec((None, kb, d), lambda i: (layer, 1, 0)),
            pl.BlockSpec((tm, d), lambda i: (i, 0)),
            pl.BlockSpec((None, 1, d), lambda i: (i // tiles_per_batch, 0, g_idx)),
        ],
        out_specs=pl.BlockSpec((tm, d), lambda i: (i, 0)),
        out_shape=jax.ShapeDtypeStruct((m, d), F32),
        compiler_params=_cparams("parallel"),
        name="out_proj",
    )(a, b, w_stack, w_stack, x2, ada3)


def _ffn_kernel(x_ref, g_ref, sc_ref, sh_ref, gate_ref, wg_ref, wu_ref, wd_ref, o_ref, h_scr, acc_scr):
    f = pl.program_id(1)

    @pl.when(f == 0)
    def _():
        h_scr[...] = _norm_mod(x_ref[...], g_ref[...], sc_ref[...], sh_ref[...]).astype(BF16)
        acc_scr[...] = jnp.zeros(acc_scr.shape, F32)

    h = h_scr[...]
    a = _silu(_dot(h, wg_ref[...])) * _dot(h, wu_ref[...])
    acc_scr[...] += _dot(a.astype(BF16), wd_ref[...])

    @pl.when(f == pl.num_programs(1) - 1)
    def _():
        o_ref[...] = x_ref[...] + gate_ref[...] * acc_scr[...]


def ffn_dense(x2, norm_g, ada3, sc_idx, sh_idx, g_idx, wg, wu, wd, layer, seq, tm=512, tf=512):
    m, d = x2.shape
    dff = wg.shape[2]
    tiles_per_batch = seq // tm
    mod = lambda k: pl.BlockSpec((None, 1, d), lambda i, f: (i // tiles_per_batch, 0, k))
    return pl.pallas_call(
        _ffn_kernel,
        grid=(m // tm, dff // tf),
        in_specs=[
            pl.BlockSpec((tm, d), lambda i, f: (i, 0)),
            pl.BlockSpec((1, d), lambda i, f: (0, 0)),
            mod(sc_idx), mod(sh_idx), mod(g_idx),
            pl.BlockSpec((None, d, tf), lambda i, f: (layer, 0, f)),
            pl.BlockSpec((None, d, tf), lambda i, f: (layer, 0, f)),
            pl.BlockSpec((None, tf, d), lambda i, f: (layer, f, 0)),
        ],
        out_specs=pl.BlockSpec((tm, d), lambda i, f: (i, 0)),
        out_shape=jax.ShapeDtypeStruct((m, d), F32),
        scratch_shapes=[pltpu.VMEM((tm, d), BF16), pltpu.VMEM((tm, d), F32)],
        compiler_params=_cparams("parallel", "arbitrary"),
        name="ffn_dense",
    )(x2, norm_g.reshape(1, d), ada3, ada3, ada3, wg, wu, wd)


def _router_kernel(x_ref, g_ref, sc_ref, sh_ref, rw_ref, cols_ref, rows_ref, counts_ref, *, n_experts):
    i = pl.program_id(0)

    @pl.when(i == 0)
    def _():
        counts_ref[...] = jnp.zeros(counts_ref.shape, F32)

    h = _norm_mod(x_ref[...], g_ref[...], sc_ref[...], sh_ref[...])
    rw = rw_ref[...]
    h_hi, rw_hi = h.astype(BF16), rw.astype(BF16)
    h_lo, rw_lo = (h - h_hi.astype(F32)).astype(BF16), (rw - rw_hi.astype(F32)).astype(BF16)
    logits = _dot(h_hi, rw_hi) + _dot(h_hi, rw_lo) + _dot(h_lo, rw_hi)
    tm = logits.shape[0]
    lane = lax.broadcasted_iota(I32, logits.shape, 1)
    lg = jnp.where(lane < n_experts, logits, NEG_INF)
    m1 = jnp.max(lg, axis=-1, keepdims=True)
    i1 = jnp.min(jnp.where(lg == m1, lane, LANES), axis=-1, keepdims=True)
    sel1 = lane == i1
    lg2 = jnp.where(sel1, NEG_INF, lg)
    m2 = jnp.max(lg2, axis=-1, keepdims=True)
    i2 = jnp.min(jnp.where(lg2 == m2, lane, LANES), axis=-1, keepdims=True)
    sel2 = lane == i2
    e = jnp.exp(m2 - m1)
    w1 = 1.0 / (1.0 + e)
    w2 = e / (1.0 + e)

    onehot = jnp.where(jnp.logical_or(sel1, sel2), 1.0, 0.0)
    tri = jnp.where(lax.broadcasted_iota(I32, (tm, tm), 0) > lax.broadcasted_iota(I32, (tm, tm), 1), 1.0, 0.0)
    rank = counts_ref[0:1, :] + _dot(tri.astype(BF16), onehot.astype(BF16))
    rank1 = jnp.sum(jnp.where(sel1, rank, 0.0), axis=-1, keepdims=True)
    rank2 = jnp.sum(jnp.where(sel2, rank, 0.0), axis=-1, keepdims=True)
    counts_ref[0:1, :] = counts_ref[0:1, :] + jnp.sum(onehot, axis=0, keepdims=True)

    cols = jnp.zeros(logits.shape, F32)
    for k, val in enumerate((rank1, rank2, i1.astype(F32), i2.astype(F32), w1, w2)):
        cols = jnp.where(lane == k, val, cols)
    cols_ref[...] = cols
    rows_ref[...] = jnp.transpose(cols)[0:SUBLANES, :].astype(I32)


def moe_router(x2, norm_g, ada3, sc_idx, sh_idx, router_w, seq, tm=512):
    m, d = x2.shape
    n_experts = router_w.shape[1]
    rw = jnp.zeros((d, LANES), F32).at[:, :n_experts].set(router_w)
    tiles_per_batch = seq // tm
    mod = lambda k: pl.BlockSpec((None, 1, d), lambda i: (i // tiles_per_batch, 0, k))
    return pl.pallas_call(
        functools.partial(_router_kernel, n_experts=n_experts),
        grid=(m // tm,),
        in_specs=[
            pl.BlockSpec((tm, d), lambda i: (i, 0)),
            pl.BlockSpec((1, d), lambda i: (0, 0)),
            mod(sc_idx), mod(sh_idx),
            pl.BlockSpec((d, LANES), lambda i: (0, 0)),
        ],
        out_specs=[
            pl.BlockSpec((tm, LANES), lambda i: (i, 0)),
            pl.BlockSpec((SUBLANES, tm), lambda i: (i, 0)),
            pl.BlockSpec((SUBLANES, LANES), lambda i: (0, 0)),
        ],
        out_shape=[
            jax.ShapeDtypeStruct((m, LANES), F32),
            jax.ShapeDtypeStruct((m // tm * SUBLANES, tm), I32),
            jax.ShapeDtypeStruct((SUBLANES, LANES), F32),
        ],
        compiler_params=_cparams("arbitrary"),
        name="moe_router",
    )(x2, norm_g.reshape(1, d), ada3, ada3, rw)


def _row_copy(src, src_row, dst, dst_row, sem):
    return pltpu.make_async_copy(src.at[pl.ds(src_row, 1)], dst.at[pl.ds(dst_row, 1)], sem)


def _dispatch_kernel(x_ref, g_ref, sc_ref, sh_ref, pos_ref, xs_in_ref, xs_ref, hp_scr, pos_smem, sem_pos, sem_rows):
    del xs_in_ref
    tm, half = hp_scr.shape
    pos_copy = pltpu.make_async_copy(pos_ref, pos_smem, sem_pos)
    pos_copy.start()
    h = _norm_mod(x_ref[...], g_ref[...], sc_ref[...], sh_ref[...])
    hp_scr[...] = _pack_bf16_pair(h[:, :half], h[:, half:])
    pos_copy.wait()

    def issue(t, carry):
        _row_copy(hp_scr, t, xs_ref, pos_smem[0, t], sem_rows).start()
        _row_copy(hp_scr, t, xs_ref, pos_smem[1, t], sem_rows).start()
        return carry

    lax.fori_loop(0, tm, issue, 0, unroll=ROW_DMA_UNROLL)

    def drain(t, carry):
        _row_copy(hp_scr, t, xs_ref, pos_smem[0, t], sem_rows).wait()
        _row_copy(hp_scr, t, xs_ref, pos_smem[1, t], sem_rows).wait()
        return carry

    lax.fori_loop(0, tm, drain, 0, unroll=ROW_DMA_UNROLL)


def moe_dispatch(x2, norm_g, ada3, sc_idx, sh_idx, pos_rows, n_rows, seq, tm=512):
    m, d = x2.shape
    half = d // 2
    tiles_per_batch = seq // tm
    mod = lambda k: pl.BlockSpec((None, 1, d), lambda i: (i // tiles_per_batch, 0, k))
    xs0 = jnp.zeros((n_rows, half), U32)
    return pl.pallas_call(
        _dispatch_kernel,
        grid=(m // tm,),
        in_specs=[
            pl.BlockSpec((tm, d), lambda i: (i, 0)),
            pl.BlockSpec((1, d), lambda i: (0, 0)),
            mod(sc_idx), mod(sh_idx),
            pl.BlockSpec((SUBLANES, tm), lambda i: (i, 0)),
            pl.BlockSpec(memory_space=pl.ANY),
        ],
        out_specs=pl.BlockSpec(memory_space=pl.ANY),
        out_shape=jax.ShapeDtypeStruct((n_rows, half), U32),
        scratch_shapes=[
            pltpu.VMEM((tm, half), U32), pltpu.SMEM((SUBLANES, tm), I32),
            pltpu.SemaphoreType.DMA, pltpu.SemaphoreType.DMA,
        ],
        input_output_aliases={5: 0},
        compiler_params=_cparams("arbitrary"),
        name="moe_dispatch",
    )(x2, norm_g.reshape(1, d), ada3, ada3, pos_rows, xs0)


def _moe_ffn_kernel(te_ref, tv_ref, xs_ref, wg_ref, wu_ref, wd_ref, ys_ref, h_scr, acc_scr):
    del te_ref
    g = pl.program_id(0)
    f = pl.program_id(1)
    tm, half = xs_ref.shape
    n_valid = tv_ref[g]
    last = f == pl.num_programs(1) - 1

    def run(n):
        @pl.when(f == 0)
        def _():
            lo, hi = _unpack_bf16_pair(xs_ref[0:n, :])
            h_scr[0:n, :half] = lo.astype(BF16)
            h_scr[0:n, half:] = hi.astype(BF16)
            acc_scr[0:n, :] = jnp.zeros((n, 2 * half), F32)

        h = h_scr[0:n, :]
        a = _silu(_dot(h, wg_ref[...])) * _dot(h, wu_ref[...])
        acc_scr[0:n, :] += _dot(a.astype(BF16), wd_ref[...])

        @pl.when(last)
        def _():
            ys_ref[0:n, :] = _pack_bf16_pair(acc_scr[0:n, :half], acc_scr[0:n, half:])
            if n < tm:
                ys_ref[n:, :] = jnp.zeros((tm - n, half), U32)

    @pl.when(n_valid > tm // 2)
    def _():
        run(tm)

    @pl.when(jnp.logical_and(n_valid > 0, n_valid <= tm // 2))
    def _():
        run(tm // 2)

    @pl.when(jnp.logical_and(n_valid == 0, last))
    def _():
        ys_ref[...] = jnp.zeros(ys_ref.shape, U32)


def moe_expert_ffn(xs, tile_expert, tile_rows, wg, wu, wd, layer, tm, tf=512):
    n_rows, half = xs.shape
    d = 2 * half
    dff = wg.shape[3]
    fidx = lambda f, tv, g: f * jnp.minimum(tv[g], 1)
    grid_spec = pltpu.PrefetchScalarGridSpec(
        num_scalar_prefetch=2,
        grid=(n_rows // tm, dff // tf),
        in_specs=[
            pl.BlockSpec((tm, half), lambda g, f, te, tv: (g, 0)),
            pl.BlockSpec((None, None, d, tf), lambda g, f, te, tv: (layer, te[g], 0, fidx(f, tv, g))),
            pl.BlockSpec((None, None, d, tf), lambda g, f, te, tv: (layer, te[g], 0, fidx(f, tv, g))),
            pl.BlockSpec((None, None, tf, d), lambda g, f, te, tv: (layer, te[g], fidx(f, tv, g), 0)),
        ],
        out_specs=pl.BlockSpec((tm, half), lambda g, f, te, tv: (g, 0)),
        scratch_shapes=[pltpu.VMEM((tm, d), BF16), pltpu.VMEM((tm, d), F32)],
    )
    return pl.pallas_call(
        _moe_ffn_kernel,
        grid_spec=grid_spec,
        out_shape=jax.ShapeDtypeStruct((n_rows, half), U32),
        compiler_params=_cparams("parallel", "arbitrary"),
        name="moe_expert_ffn",
    )(tile_expert, tile_rows, xs, wg, wu, wd)


def _combine_kernel(x_ref, gate_ref, cols_ref, pos_ref, ys_ref, o_ref, r1_scr, r2_scr, pos_smem, sem_pos, sem_rows):
    tm, half = r1_scr.shape
    pos_copy = pltpu.make_async_copy(pos_ref, pos_smem, sem_pos)
    pos_copy.start()
    pos_copy.wait()

    def issue(t, carry):
        _row_copy(ys_ref, pos_smem[0, t], r1_scr, t, sem_rows).start()
        _row_copy(ys_ref, pos_smem[1, t], r2_scr, t, sem_rows).start()
        return carry

    lax.fori_loop(0, tm, issue, 0, unroll=ROW_DMA_UNROLL)

    def drain(t, carry):
        _row_copy(ys_ref, pos_smem[0, t], r1_scr, t, sem_rows).wait()
        _row_copy(ys_ref, pos_smem[1, t], r2_scr, t, sem_rows).wait()
        return carry

    lax.fori_loop(0, tm, drain, 0, unroll=ROW_DMA_UNROLL)

    w1 = cols_ref[:, 4:5]
    w2 = cols_ref[:, 5:6]
    lo1, hi1 = _unpack_bf16_pair(r1_scr[...])
    lo2, hi2 = _unpack_bf16_pair(r2_scr[...])
    o_ref[:, :half] = x_ref[:, :half] + gate_ref[:, :half] * (w1 * lo1 + w2 * lo2)
    o_ref[:, half:] = x_ref[:, half:] + gate_ref[:, half:] * (w1 * hi1 + w2 * hi2)


def moe_combine(x2, ada3, g_idx, cols, pos_rows, ys, seq, tm=512):
    m, d = x2.shape
    half = d // 2
    tiles_per_batch = seq // tm
    return pl.pallas_call(
        _combine_kernel,
        grid=(m // tm,),
        in_specs=[
            pl.BlockSpec((tm, d), lambda i: (i, 0)),
            pl.BlockSpec((None, 1, d), lambda i: (i // tiles_per_batch, 0, g_idx)),
            pl.BlockSpec((tm, LANES), lambda i: (i, 0)),
            pl.BlockSpec((SUBLANES, tm), lambda i: (i, 0)),
            pl.BlockSpec(memory_space=pl.ANY),
        ],
        out_specs=pl.BlockSpec((tm, d), lambda i: (i, 0)),
        out_shape=jax.ShapeDtypeStruct((m, d), F32),
        scratch_shapes=[
            pltpu.VMEM((tm, half), U32), pltpu.VMEM((tm, half), U32), pltpu.SMEM((SUBLANES, tm), I32),
            pltpu.SemaphoreType.DMA, pltpu.SemaphoreType.DMA,
        ],
        compiler_params=_cparams("arbitrary"),
        name="moe_combine",
    )(x2, ada3, cols, pos_rows, ys)


def moe_ffn(x2, norm_g, ada3, sc_idx, sh_idx, g_idx, router_w, wg, wu, wd, layer, seq, tm=512,
            row_tile=MOE_ROW_TILE):
    m = x2.shape[0]
    n_experts = router_w.shape[1]
    row_tile = min(row_tile, m)
    cols, rows, counts = moe_router(x2, norm_g, ada3, sc_idx, sh_idx, router_w, seq, tm=tm)
    n_rows = 2 * m + n_experts * row_tile
    counts = counts[0, :n_experts].astype(I32)
    padded = ((counts + row_tile - 1) // row_tile) * row_tile
    ends = jnp.cumsum(padded)
    offs = ends - padded
    rows3 = rows.reshape(m // tm, SUBLANES, tm)
    pos_rows = rows3.at[:, 0].add(offs[rows3[:, 2]]).at[:, 1].add(offs[rows3[:, 3]]).reshape(rows.shape)
    tile_start = jnp.arange(n_rows // row_tile, dtype=I32) * row_tile
    tile_expert = jnp.minimum(jnp.sum(tile_start[:, None] >= ends[None, :], axis=1), n_experts - 1).astype(I32)
    tile_rows = jnp.clip((offs + counts)[tile_expert] - tile_start, 0, row_tile).astype(I32)

    xs = moe_dispatch(x2, norm_g, ada3, sc_idx, sh_idx, pos_rows, n_rows, seq, tm=tm)
    ys = moe_expert_ffn(xs, tile_expert, tile_rows, wg, wu, wd, layer, row_tile)
    return moe_combine(x2, ada3, g_idx, cols, pos_rows, ys, seq, tm=tm)


def _head_gain(q_g, k_g, n_q_heads, n_k_heads, q_scale):
    return jnp.concatenate([jnp.tile(q_g * q_scale, n_q_heads), jnp.tile(k_g, n_k_heads)]).reshape(1, -1).astype(F32)


def kernel(x, c, ada_w, ada_b, norm_mix_g, norm_ffn_g, ev_w_in, ev_w_out, a_q_norm_g, a_k_norm_g, a_lambda,
           a_head_norm_g, ffn_w_gate, ffn_w_up, ffn_w_down, od_w_in, od_w_out, c_q_norm_g, c_k_norm_g,
           d_conv_w, d_conv_b, d_ln_g, d_ln_b, moe_router_w, moe_w_gate, moe_w_up, moe_w_down):
    batch, seq, d = x.shape
    depth = ada_w.shape[0]
    m = batch * seq
    scale = HEAD_DIM ** -0.5
    a_width = ev_w_out.shape[1] // 2
    a_heads = a_width // (2 * HEAD_DIM)
    b_heads = a_width // HEAD_DIM
    c_width = od_w_out.shape[1] // 2
    c_heads = c_width // HEAD_DIM

    ev_w_in, ev_w_out, od_w_in, od_w_out = (w.astype(BF16) for w in (ev_w_in, ev_w_out, od_w_in, od_w_out))
    ffn_w = tuple(w.astype(BF16) for w in (ffn_w_gate, ffn_w_up, ffn_w_down))
    moe_w = tuple(w.astype(BF16) for w in (moe_w_gate, moe_w_up, moe_w_down))

    ada = ada_all(c, ada_w, ada_b)
    x2 = x.reshape(m, d)
    for i in range(depth):
        j = i // 2
        ada3 = ada[i].reshape(batch, 1, 6 * d)
        if i % 2 == 0:
            lam_init = 0.8 - 0.6 * math.exp(-0.3 * i)
            lv = a_lambda[j].astype(F32)
            lam = jnp.exp(jnp.sum(lv[0] * lv[1])) - jnp.exp(jnp.sum(lv[2] * lv[3])) + lam_init
            slopes = 2.0 ** (-8.0 * jnp.arange(1, a_heads + 1, dtype=F32) / a_heads)
            gain = _head_gain(a_q_norm_g[j], a_k_norm_g[j], 2 * a_heads, 2 * a_heads, scale * LOG2E)
            (y,) = in_proj(x2, norm_mix_g[i], ada3, 1, 0, ev_w_in, j, gain, seq, tn=2 * a_width)
            oa = diff_attention(y, slopes, lam.reshape(1), a_head_norm_g[j], batch, seq, a_heads,
                                k_col0=a_width, v_col0=2 * a_width, out_scale=1.0 - lam_init)
            ob = stick_attention(y, batch, seq, b_heads, q_col0=3 * a_width, k_col0=4 * a_width,
                                 v_col0=5 * a_width)
            x2 = out_proj(oa, ob, ev_w_out, j, x2, ada3, 2, seq)
            x2 = ffn_dense(x2, norm_ffn_g[i], ada3, 4, 3, 5, *ffn_w, j, seq)
        else:
            gain = _head_gain(c_q_norm_g[j], c_k_norm_g[j], c_heads, c_heads, scale)
            qkv, du = in_proj(x2, norm_mix_g[i], ada3, 1, 0, od_w_in, j, gain, seq, n_first=3 * c_width,
                              perm_first=True)
            oc = dilated_mixture(qkv, batch, seq, c_heads)
            od = conformer_conv(du, d_conv_w[j], d_conv_b[j], d_ln_g[j], d_ln_b[j], batch, seq, a_col0=0)
            x2 = out_proj(oc, od, od_w_out, j, x2, ada3, 2, seq, a_residue_major=True)
            x2 = moe_ffn(x2, norm_ffn_g[i], ada3, 4, 3, 5, moe_router_w[j], *moe_w, j, seq)
    return x2.reshape(batch, seq, d)
```
